```python
import jax, jax.numpy as jnp
from jax import lax
import numpy as np

D_MODEL = 1024
BATCH = 4
SEQ = 4096
DEPTH = 4

PLE_DIM = 256
GLA_HEADS = 4
GLA_DK = D_MODEL // 8
GLA_DV = D_MODEL // 4
GLA_GATE_RANK = 16
GLA_GATE_TAU = 16.0
GLA_CHUNK = 64
MOBA_HEADS = 8
MOBA_DH = D_MODEL // 8
MOBA_BLOCK = 256
MOBA_TOPK = 3
MOBA_QCHUNK = 32
D_FF = 4 * D_MODEL
EPS = 1e-6

IN_SPLITS = (
    GLA_HEADS * GLA_DK,
    GLA_HEADS * GLA_DK,
    GLA_HEADS * GLA_DV,
    GLA_GATE_RANK,
    GLA_HEADS * GLA_DV,
    MOBA_HEADS * MOBA_DH,
    MOBA_HEADS * MOBA_DH,
    MOBA_HEADS * MOBA_DH,
    D_MODEL,
    D_MODEL,
)
N_IN = sum(IN_SPLITS)

kernel_name = 'hybrid_gla_moba_gated_block'


def rms_norm(x, g):
    xf = x.astype(jnp.float32)
    y = xf * lax.rsqrt(jnp.mean(xf * xf, axis=-1, keepdims=True) + EPS)
    return (y * g.astype(jnp.float32)).astype(x.dtype)


def gla_chunked(q, k, v, log_a):
    B, S, H, dk = q.shape
    dv = v.shape[-1]
    n = S // GLA_CHUNK

    def chunk(t):
        return t.astype(jnp.float32).reshape(B, n, GLA_CHUNK, H, t.shape[-1]).transpose(0, 3, 1, 2, 4)

    q, k, v, log_a = chunk(q), chunk(k), chunk(v), chunk(log_a)
    b = jnp.cumsum(log_a, axis=3)
    b_last = b[:, :, :, -1:, :]
    qd = q * jnp.exp(b) * (dk ** -0.5)
    kd = k * jnp.exp(-b)
    kl = k * jnp.exp(b_last - b)
    kv = jnp.einsum('bhncd,bhnce->bhnde', kl, v)
    decay = jnp.exp(b_last[:, :, :, 0, :])

    def step(state, inp):
        dec, kv_n = inp
        return dec[..., None] * state + kv_n, state

    s0 = jnp.zeros((B, H, dk, dv), jnp.float32)
    _, s_prev = lax.scan(step, s0, (jnp.moveaxis(decay, 2, 0), jnp.moveaxis(kv, 2, 0)))
    s_prev = jnp.moveaxis(s_prev, 0, 2)
    o_inter = jnp.einsum('bhncd,bhnde->bhnce', qd, s_prev)
    causal = jnp.tril(jnp.ones((GLA_CHUNK, GLA_CHUNK), bool))
    a = jnp.where(causal, jnp.einsum('bhncd,bhnsd->bhncs', qd, kd), 0.0)
    o = o_inter + jnp.einsum('bhncs,bhnse->bhnce', a, v)
    return o.transpose(0, 2, 3, 1, 4).reshape(B, S, H, dv)


def moba_attention(q, k, v):
    B, S, H, dh = q.shape
    s_pad = -(-S // MOBA_BLOCK) * MOBA_BLOCK
    nb = s_pad // MOBA_BLOCK
    n_sel = min(MOBA_TOPK, max(nb - 1, 1))
    nq = s_pad // MOBA_QCHUNK
    pad = ((0, 0), (0, s_pad - S), (0, 0), (0, 0))
    q, k, v = [jnp.pad(t, pad).transpose(0, 2, 1, 3) for t in (q, k, v)]
    kb = k.reshape(B, H, nb, MOBA_BLOCK, dh)
    vb = v.reshape(B, H, nb, MOBA_BLOCK, dh)
    scale = dh ** -0.5
    slopes = 2.0 ** (-8.0 * jnp.arange(1, H + 1, dtype=jnp.float32) / H)

    k_mean = jnp.mean(kb.astype(jnp.float32), axis=3)
    gate = jnp.einsum('bhsd,bhnd->bhsn', q.astype(jnp.float32), k_mean)
    pos = jnp.arange(s_pad)
    fully_past = jnp.arange(nb)[None, :] < (pos // MOBA_BLOCK)[:, None]
    gate = jnp.where(fully_past, gate, -jnp.inf)
    top_val, top_idx = lax.top_k(gate, n_sel)
    valid = jnp.isfinite(top_val)

    def to_chunks(t):
        return jnp.moveaxis(t.reshape(B, H, nq, MOBA_QCHUNK, *t.shape[3:]), 2, 0)

    b_ix = jnp.arange(B)[:, None, None, None]
    h_ix = jnp.arange(H)[None, :, None, None]

    def attend(args):
        q_c, idx_c, valid_c, c = args
        t = c * MOBA_QCHUNK + jnp.arange(MOBA_QCHUNK)
        blk = (c * MOBA_QCHUNK) // MOBA_BLOCK
        k_own = lax.dynamic_index_in_dim(kb, blk, axis=2, keepdims=False)
        v_own = lax.dynamic_index_in_dim(vb, blk, axis=2, keepdims=False)
        own_pos = blk * MOBA_BLOCK + jnp.arange(MOBA_BLOCK)
        d_own = (t[:, None] - own_pos[None, :]).astype(jnp.float32)
        sc_own = (jnp.einsum('bhqd,bhkd->bhqk', q_c, k_own).astype(jnp.float32) * scale
                  - slopes[:, None, None] * jnp.abs(d_own))
        sc_own = jnp.where(d_own >= 0, sc_own, -jnp.inf)
        k_sel = kb[b_ix, h_ix, idx_c]
        v_sel = vb[b_ix, h_ix, idx_c]
        sel_pos = idx_c[..., None] * MOBA_BLOCK + jnp.arange(MOBA_BLOCK)
        d_sel = (t[:, None, None] - sel_pos).astype(jnp.float32)
        sc_sel = (jnp.einsum('bhqd,bhqnkd->bhqnk', q_c, k_sel).astype(jnp.float32) * scale
                  - slopes[:, None, None, None] * jnp.abs(d_sel))
        sc_sel = jnp.where(valid_c[..., None], sc_sel, -jnp.inf)
        scores = jnp.concatenate(
            [sc_own, sc_sel.reshape(B, H, MOBA_QCHUNK, n_sel * MOBA_BLOCK)], axis=-1)
        prob = jax.nn.softmax(scores, axis=-1).astype(v.dtype)
        p_own = prob[..., :MOBA_BLOCK]
        p_sel = prob[..., MOBA_BLOCK:].reshape(B, H, MOBA_QCHUNK, n_sel, MOBA_BLOCK)
        return (jnp.einsum('bhqk,bhkd->bhqd', p_own, v_own)
                + jnp.einsum('bhqnk,bhqnkd->bhqd', p_sel, v_sel))

    out = lax.map(attend, (to_chunks(q), to_chunks(top_idx), to_chunks(valid), jnp.arange(nq)))
    out = jnp.moveaxis(out, 0, 2).reshape(B, H, s_pad, dh)[:, :, :S]
    return out.transpose(0, 2, 1, 3).reshape(B, S, H * dh)


def setup_inputs(seed: int = 0) -> dict:
    key = jax.random.key(seed)
    ks = jax.random.split(key, 20)

    def nrm(k, shape, fan_in, scale=1.0):
        return jax.random.normal(k, shape, jnp.float32) * (scale * fan_in ** -0.5)

    def gain(k, shape):
        return 1.0 + 0.05 * jax.random.normal(k, shape, jnp.float32)

    return {
        'x': jax.random.normal(ks[0], (BATCH, SEQ, D_MODEL), jnp.float32),
        'p': jax.random.normal(ks[1], (DEPTH, BATCH, SEQ, PLE_DIM), jnp.float32),
        'norm_mix': gain(ks[2], (DEPTH, D_MODEL)),
        'w_in': nrm(ks[3], (DEPTH, D_MODEL, N_IN), D_MODEL),
        'gla_gate_w2': nrm(ks[4], (DEPTH, GLA_GATE_RANK, GLA_HEADS * GLA_DK), GLA_GATE_RANK),
        'gla_gate_b': 0.1 * jax.random.normal(ks[5], (DEPTH, GLA_HEADS * GLA_DK), jnp.float32),
        'gla_out_norm': gain(ks[6], (DEPTH, GLA_HEADS, GLA_DV)),
        'moba_q_norm': gain(ks[7], (DEPTH, MOBA_DH)),
        'moba_k_norm': gain(ks[8], (DEPTH, MOBA_DH)),
        'w_branch_a': nrm(ks[9], (DEPTH, GLA_HEADS * GLA_DV, D_MODEL), GLA_HEADS * GLA_DV),
        'w_branch_b': nrm(ks[10], (DEPTH, MOBA_HEADS * MOBA_DH, D_MODEL), MOBA_HEADS * MOBA_DH),
        'w_out': nrm(ks[11], (DEPTH, D_MODEL, D_MODEL), D_MODEL, 0.5),
        'norm_mlp': gain(ks[12], (DEPTH, D_MODEL)),
        'w_up': nrm(ks[13], (DEPTH, D_MODEL, D_FF), D_MODEL),
        'w_down': nrm(ks[14], (DEPTH, D_FF, D_MODEL), D_FF, 0.5),
        'norm_ple': gain(ks[15], (DEPTH, D_MODEL)),
        'w_ple_gate': nrm(ks[16], (DEPTH, D_MODEL, D_MODEL), D_MODEL),
        'w_ple': nrm(ks[17], (DEPTH, PLE_DIM, D_MODEL), PLE_DIM, 0.5),
    }


def reference(x, p, norm_mix, w_in, gla_gate_w2, gla_gate_b, gla_out_norm, moba_q_norm,
              moba_k_norm, w_branch_a, w_branch_b, w_out, norm_mlp, w_up, w_down,
              norm_ple, w_ple_gate, w_ple):
    B, S, _ = x.shape
    points = []
    acc = 0
    for sz in IN_SPLITS[:-1]:
        acc += sz
        points.append(acc)

    for i in range(DEPTH):
        h = rms_norm(x, norm_mix[i])
        u = h @ w_in[i]
        (gq, gk, gv, g_lr, g_r, mq, mk, mv, gate_a, gate_b) = jnp.split(u, points, axis=-1)

        z = (g_lr @ gla_gate_w2[i] + gla_gate_b[i]).astype(jnp.float32)
        log_a = jax.nn.log_sigmoid(z) / GLA_GATE_TAU
        o_a = gla_chunked(gq.reshape(B, S, GLA_HEADS, GLA_DK),
                          gk.reshape(B, S, GLA_HEADS, GLA_DK),
                          gv.reshape(B, S, GLA_HEADS, GLA_DV),
                          log_a.reshape(B, S, GLA_HEADS, GLA_DK)).astype(x.dtype)
        o_a = rms_norm(o_a, gla_out_norm[i]).reshape(B, S, GLA_HEADS * GLA_DV) * jax.nn.silu(g_r)
        y_a = o_a @ w_branch_a[i]

        qh = rms_norm(mq.reshape(B, S, MOBA_HEADS, MOBA_DH), moba_q_norm[i])
        kh = rms_norm(mk.reshape(B, S, MOBA_HEADS, MOBA_DH), moba_k_norm[i])
        vh = mv.reshape(B, S, MOBA_HEADS, MOBA_DH)
        y_b = moba_attention(qh, kh, vh) @ w_branch_b[i]

        y = jax.nn.sigmoid(gate_a) * y_a + jax.nn.sigmoid(gate_b) * y_b
        x = x + y @ w_out[i]

        h2 = rms_norm(x, norm_mlp[i])
        x = x + jnp.square(jax.nn.relu(h2 @ w_up[i])) @ w_down[i]

        ple_gate = jax.nn.sigmoid(rms_norm(x, norm_ple[i]) @ w_ple_gate[i])
        x = x + ple_gate * (p[i] @ w_ple[i])
    return x
```

```python
import functools

import jax
import jax.numpy as jnp
from jax import lax
from jax.experimental import pallas as pl
from jax.experimental.pallas import tpu as pltpu

F32 = jnp.float32
BF16 = jnp.bfloat16

EPS = 1e-6
GLA_HEADS = 4
GLA_DK = 128
GLA_DV = 256
GLA_GATE_RANK = 16
GLA_GATE_TAU = 16.0
GLA_CHUNK = 64
MOBA_HEADS = 8
MOBA_DH = 128
MOBA_BLOCK = 256
MOBA_TOPK = 3

LANES = 128
VMEM_LIMIT = 48 * 1024 * 1024
NEG_BIG = -1e30

NT_DIMS = (((1,), (1,)), ((), ()))
TN_DIMS = (((0,), (0,)), ((), ()))


def _cparams(*sem):
    return pltpu.CompilerParams(dimension_semantics=sem, vmem_limit_bytes=VMEM_LIMIT)


def _rms(x, g):
    return x * lax.rsqrt(jnp.mean(x * x, axis=-1, keepdims=True) + EPS) * g


def _sigmoid(x):
    return 1.0 / (1.0 + jnp.exp(-x))


def _norm_kernel(x_ref, g_ref, o_ref):
    o_ref[...] = _rms(x_ref[...], g_ref[...]).astype(o_ref.dtype)


def _norm_cast(x, g, tm=1024):
    t, d = x.shape
    return pl.pallas_call(
        _norm_kernel,
        grid=(t // tm,),
        in_specs=[pl.BlockSpec((tm, d), lambda i: (i, 0)),
                  pl.BlockSpec((1, d), lambda i: (0, 0))],
        out_specs=pl.BlockSpec((tm, d), lambda i: (i, 0)),
        out_shape=jax.ShapeDtypeStruct((t, d), BF16),
        compiler_params=_cparams("parallel"),
        name="norm_cast",
    )(x, g.reshape(1, d))


def _proj_kernel(h_ref, w_ref, o_ref, *, act):
    acc = jnp.dot(h_ref[...], w_ref[...], preferred_element_type=F32)
    if act == "silu":
        acc = acc * _sigmoid(acc)
    elif act == "sigmoid":
        acc = _sigmoid(acc)
    o_ref[...] = acc.astype(o_ref.dtype)


def _proj(h, w, out_dtype, act=None, tm=512, tn=1024, name="proj"):
    t, d = h.shape
    n = w.shape[1]
    return pl.pallas_call(
        functools.partial(_proj_kernel, act=act),
        grid=(n // tn, t // tm),
        in_specs=[pl.BlockSpec((tm, d), lambda j, i: (i, 0)),
                  pl.BlockSpec((d, tn), lambda j, i: (0, j))],
        out_specs=pl.BlockSpec((tm, tn), lambda j, i: (i, j)),
        out_shape=jax.ShapeDtypeStruct((t, n), out_dtype),
        compiler_params=_cparams("parallel", "parallel"),
        name=name,
    )(h, w)


def _qk_proj_kernel(h_ref, w_ref, g_ref, s_ref, o_ref):
    acc = jnp.dot(h_ref[...], w_ref[...], preferred_element_type=F32)
    g = g_ref[...] * s_ref[...]
    for hh in range(acc.shape[1] // MOBA_DH):
        seg = acc[:, hh * MOBA_DH:(hh + 1) * MOBA_DH]
        o_ref[:, hh * MOBA_DH:(hh + 1) * MOBA_DH] = _rms(seg, g).astype(o_ref.dtype)


def _qk_proj(h, w, gains, scales, tm=512):
    t, d = h.shape
    n = w.shape[1]
    tn = n // 2
    return pl.pallas_call(
        _qk_proj_kernel,
        grid=(2, t // tm),
        in_specs=[pl.BlockSpec((tm, d), lambda j, i: (i, 0)),
                  pl.BlockSpec((d, tn), lambda j, i: (0, j)),
                  pl.BlockSpec((None, 1, MOBA_DH), lambda j, i: (j, 0, 0)),
                  pl.BlockSpec((None, 1, MOBA_DH), lambda j, i: (j, 0, 0))],
        out_specs=pl.BlockSpec((tm, tn), lambda j, i: (i, j)),
        out_shape=jax.ShapeDtypeStruct((t, n), BF16),
        compiler_params=_cparams("parallel", "parallel"),
        name="moba_qk_proj",
    )(h, w, gains, scales)


def _loga_kernel(h_ref, wlr_ref, w2_ref, b_ref, o_ref):
    lr = jnp.dot(h_ref[...], wlr_ref[...], preferred_element_type=F32)
    z = jnp.dot(lr.astype(BF16), w2_ref[...], preferred_element_type=F32) + b_ref[...]
    log_sig = jnp.minimum(z, 0.0) - jnp.log(1.0 + jnp.exp(-jnp.abs(z)))
    o_ref[...] = log_sig * (1.0 / GLA_GATE_TAU)


def _loga_proj(h, wlr, w2, b, tm=1024):
    t, d = h.shape
    r = wlr.shape[1]
    n = w2.shape[1]
    return pl.pallas_call(
        _loga_kernel,
        grid=(t // tm,),
        in_specs=[pl.BlockSpec((tm, d), lambda i: (i, 0)),
                  pl.BlockSpec((d, r), lambda i: (0, 0)),
                  pl.BlockSpec((r, n), lambda i: (0, 0)),
                  pl.BlockSpec((1, n), lambda i: (0, 0))],
        out_specs=pl.BlockSpec((tm, n), lambda i: (i, 0)),
        out_shape=jax.ShapeDtypeStruct((t, n), F32),
        compiler_params=_cparams("parallel"),
        name="gla_loga_proj",
    )(h, wlr, w2, b)


def _gla_kernel(q_ref, k_ref, v_ref, la_ref, gr_ref, gn_ref, o_ref, st_ref, *, nchunk):
    c = GLA_CHUNK

    @pl.when(pl.program_id(2) == 0)
    def _():
        st_ref[...] = jnp.zeros_like(st_ref)

    row = lax.broadcasted_iota(jnp.int32, (c, c), 0)
    col = lax.broadcasted_iota(jnp.int32, (c, c), 1)
    causal = row >= col
    tri = causal.astype(BF16)
    gn = gn_ref[...]

    def body(ci, carry):
        rows = pl.ds(pl.multiple_of(ci * c, c), c)
        la = la_ref[rows, :]
        la_hi = la.astype(BF16)
        la_lo = (la - la_hi.astype(F32)).astype(BF16)
        b = (jnp.dot(tri, la_hi, preferred_element_type=F32)
             + jnp.dot(tri, la_lo, preferred_element_type=F32))
        b_last = b[c - 1:c, :]
        q = q_ref[rows, :]
        k = k_ref[rows, :]
        v = v_ref[rows, :]
        qd = (q * jnp.exp(b) * (GLA_DK ** -0.5)).astype(BF16)
        kd = (k * jnp.exp(-b)).astype(BF16)
        kl = (k * jnp.exp(b_last - b)).astype(BF16)
        st = st_ref[...]
        o = lax.dot_general(qd, st.astype(BF16), NT_DIMS, preferred_element_type=F32)
        a = lax.dot_general(qd, kd, NT_DIMS, preferred_element_type=F32)
        a = jnp.where(causal, a, 0.0)
        o = o + jnp.dot(a.astype(BF16), v, preferred_element_type=F32)
        kv_t = lax.dot_general(v, kl, TN_DIMS, preferred_element_type=F32)
        st_ref[...] = st * jnp.exp(b_last) + kv_t
        y = _rms(o, gn) * gr_ref[rows, :]
        o_ref[rows, :] = y.astype(o_ref.dtype)
        return carry

    lax.fori_loop(0, nchunk, body, 0)


def _gla(gqk, gv, log_a, gr, gnorm, batch, seq, rb=512):
    t = gqk.shape[0]
    nblk = seq // rb
    h = GLA_HEADS
    row_map = lambda b, hh, s: (b * nblk + s, hh)
    return pl.pallas_call(
        functools.partial(_gla_kernel, nchunk=rb // GLA_CHUNK),
        grid=(batch, h, nblk),
        in_specs=[pl.BlockSpec((rb, GLA_DK), row_map),
                  pl.BlockSpec((rb, GLA_DK), lambda b, hh, s: (b * nblk + s, h + hh)),
                  pl.BlockSpec((rb, GLA_DV), row_map),
                  pl.BlockSpec((rb, GLA_DK), row_map),
                  pl.BlockSpec((rb, GLA_DV), row_map),
                  pl.BlockSpec((None, 1, GLA_DV), lambda b, hh, s: (hh, 0, 0))],
        out_specs=pl.BlockSpec((rb, GLA_DV), row_map),
        out_shape=jax.ShapeDtypeStruct((t, h * GLA_DV), BF16),
        scratch_shapes=[pltpu.VMEM((GLA_DV, GLA_DK), F32)],
        compiler_params=_cparams("parallel", "parallel", "arbitrary"),
        name="gla_scan",
    )(gqk, gqk, gv, log_a, gr, gnorm.reshape(h, 1, GLA_DV))


def _moba_kernel(q_ref, k_ref, v_ref, slope_ref, o_ref, kaug_ref, kmh_ref, kml_ref, *, nb):
    blk, dh = MOBA_BLOCK, MOBA_DH
    i = pl.program_id(2)
    slope = slope_ref[...]

    @pl.when(i == 0)
    def _():
        c = lax.broadcasted_iota(jnp.int32, (blk, dh), 1)
        r = lax.broadcasted_iota(jnp.int32, (blk, dh), 0).astype(F32)
        sl = slope[:, :dh]
        means = []
        for n in range(nb):
            kn = k_ref[n * blk:(n + 1) * blk, :]
            kaug_ref[n * blk:(n + 1) * blk, :dh] = kn
            e = jnp.where(c == n, 1.0, 0.0)
            e = jnp.where((c == nb) | (c == nb + 2), 1.0, e)
            e = jnp.where(c == nb + 1, sl * r, e)
            e = jnp.where(c == nb + 3, sl * float(n * blk), e)
            kaug_ref[n * blk:(n + 1) * blk, dh:] = e.astype(BF16)
            means.append(jnp.mean(kn.astype(F32), axis=0, keepdims=True))
        km = jnp.concatenate(means, axis=0)
        km_hi = km.astype(BF16)
        kmh_ref[...] = km_hi
        kml_ref[...] = (km - km_hi.astype(F32)).astype(BF16)

    q = q_ref[...]
    g_t = (lax.dot_general(kmh_ref[...], q, NT_DIMS, preferred_element_type=F32)
           + lax.dot_general(kml_ref[...], q, NT_DIMS, preferred_element_type=F32))
    nidx = lax.broadcasted_iota(jnp.int32, (nb, blk), 0)
    valid = nidx < i
    g = jnp.where(valid, g_t, -jnp.inf)
    rank = jnp.zeros((nb, blk), jnp.int32)
    for m in range(nb):
        gm = g[m:m + 1, :]
        beats = (gm > g) | ((gm == g) & (nidx > m))
        rank = rank + beats.astype(jnp.int32)
    sel = (valid & (rank < MOBA_TOPK)) | (nidx == i)
    selb = jnp.where(sel, 0.0, NEG_BIG)

    ridx = lax.broadcasted_iota(jnp.int32, (dh - nb, blk), 0) + nb
    t_rel = lax.broadcasted_iota(jnp.int32, (dh - nb, blk), 1).astype(F32)
    i_f = i.astype(F32)
    rest = jnp.where(ridx == nb, -slope * t_rel, 0.0)
    rest = jnp.where((ridx == nb + 1) | (ridx == nb + 3), 1.0, rest)
    rest = jnp.where(ridx == nb + 2, -slope * (i_f * float(blk)), rest)
    x_t = jnp.concatenate([selb, rest], axis=0)
    q_aug = jnp.concatenate([q, x_t.T.astype(BF16)], axis=1)

    def scores(j):
        rows = pl.ds(pl.multiple_of(j * blk, blk), blk)
        s = lax.dot_general(q_aug, kaug_ref[rows, :], NT_DIMS, preferred_element_type=F32)
        return s, v_ref[rows, :]

    s, vj = scores(i)
    rr = lax.broadcasted_iota(jnp.int32, (blk, blk), 0)
    cc = lax.broadcasted_iota(jnp.int32, (blk, blk), 1)
    s = jnp.where(rr >= cc, s, -jnp.inf)
    m0 = jnp.max(s, axis=-1, keepdims=True)
    p = jnp.exp(s - m0)
    l0 = jnp.sum(p, axis=-1, keepdims=True)
    acc0 = jnp.dot(p.astype(BF16), vj, preferred_element_type=F32)

    def body(j, carry):
        m, l, acc = carry
        s, vj = scores(j)
        m_new = jnp.maximum(m, jnp.max(s, axis=-1, keepdims=True))
        alpha = jnp.exp(m - m_new)
        p = jnp.exp(s - m_new)
        l = alpha * l + jnp.sum(p, axis=-1, keepdims=True)
        acc = alpha * acc + jnp.dot(p.astype(BF16), vj, preferred_element_type=F32)
        return m_new, l, acc

    _, l, acc = lax.fori_loop(0, i, body, (m0, l0, acc0))
    o_ref[...] = (acc / l).astype(o_ref.dtype)


def _moba(qk, v, slopes, batch, seq):
    t = qk.shape[0]
    h, dh, blk = MOBA_HEADS, MOBA_DH, MOBA_BLOCK
    nb = seq // blk
    return pl.pallas_call(
        functools.partial(_moba_kernel, nb=nb),
        grid=(batch, h, nb),
        in_specs=[pl.BlockSpec((blk, dh), lambda b, hh, i: (b * nb + i, hh)),
                  pl.BlockSpec((seq, dh), lambda b, hh, i: (b, h + hh)),
                  pl.BlockSpec((seq, dh), lambda b, hh, i: (b, hh)),
                  pl.BlockSpec((None, 1, blk), lambda b, hh, i: (hh, 0, 0))],
        out_specs=pl.BlockSpec((blk, dh), lambda b, hh, i: (b * nb + i, hh)),
        out_shape=jax.ShapeDtypeStruct((t, h * dh), BF16),
        scratch_shapes=[pltpu.VMEM((seq, 2 * dh), BF16),
                        pltpu.VMEM((nb, dh), BF16),
                        pltpu.VMEM((nb, dh), BF16)],
        compiler_params=_cparams("parallel", "parallel", "arbitrary"),
        name="moba_attn",
    )(qk, qk, v, slopes)


def _merge_kernel(x_ref, oa_ref, ob_ref, gate_ref, wa_ref, wb_ref, wo_ref, o_ref):
    d = x_ref.shape[1]
    ya = jnp.dot(oa_ref[...], wa_ref[...], preferred_element_type=F32)
    yb = jnp.dot(ob_ref[...], wb_ref[...], preferred_element_type=F32)
    y = gate_ref[:, :d] * ya + gate_ref[:, d:] * yb
    o_ref[...] = x_ref[...] + jnp.dot(y.astype(BF16), wo_ref[...], preferred_element_type=F32)


def _merge(x, oa, ob, gates, wa, wb, wo, tm=512):
    t, d = x.shape
    row = lambda i: (i, 0)
    full = lambda i: (0, 0)
    return pl.pallas_call(
        _merge_kernel,
        grid=(t // tm,),
        in_specs=[pl.BlockSpec((tm, d), row), pl.BlockSpec((tm, d), row),
                  pl.BlockSpec((tm, d), row), pl.BlockSpec((tm, 2 * d), row),
                  pl.BlockSpec((d, d), full), pl.BlockSpec((d, d), full),
                  pl.BlockSpec((d, d), full)],
        out_specs=pl.BlockSpec((tm, d), row),
        out_shape=jax.ShapeDtypeStruct((t, d), F32),
        compiler_params=_cparams("parallel"),
        name="merge_out_proj",
    )(x, oa, ob, gates, wa, wb, wo)


def _mlp_kernel(x_ref, g_ref, wu_ref, wd_ref, o_ref, h_ref, acc_ref):
    f = pl.program_id(1)

    @pl.when(f == 0)
    def _():
        h_ref[...] = _rms(x_ref[...], g_ref[...]).astype(BF16)
        acc_ref[...] = jnp.zeros_like(acc_ref)

    up = jnp.dot(h_ref[...], wu_ref[...], preferred_element_type=F32)
    act = jnp.square(jnp.maximum(up, 0.0)).astype(BF16)
    acc_ref[...] += jnp.dot(act, wd_ref[...], preferred_element_type=F32)

    @pl.when(f == pl.num_programs(1) - 1)
    def _():
        o_ref[...] = x_ref[...] + acc_ref[...]


def _mlp(x, g, wu, wd, tm=512, tf=1024):
    t, d = x.shape
    ff = wu.shape[1]
    return pl.pallas_call(
        _mlp_kernel,
        grid=(t // tm, ff // tf),
        in_specs=[pl.BlockSpec((tm, d), lambda i, f: (i, 0)),
                  pl.BlockSpec((1, d), lambda i, f: (0, 0)),
                  pl.BlockSpec((d, tf), lambda i, f: (0, f)),
                  pl.BlockSpec((tf, d), lambda i, f: (f, 0))],
        out_specs=pl.BlockSpec((tm, d), lambda i, f: (i, 0)),
        out_shape=jax.ShapeDtypeStruct((t, d), F32),
        scratch_shapes=[pltpu.VMEM((tm, d), BF16), pltpu.VMEM((tm, d), F32)],
        compiler_params=_cparams("parallel", "arbitrary"),
        name="mlp_relu2",
    )(x, g.reshape(1, d), wu, wd)


def _ple_kernel(x_ref, p_ref, g_ref, wg_ref, wp_ref, gn_ref, o_ref, *maybe_h_ref):
    x = x_ref[...]
    hn = _rms(x, g_ref[...]).astype(BF16)
    gate = _sigmoid(jnp.dot(hn, wg_ref[...], preferred_element_type=F32))
    e = jnp.dot(p_ref[...].astype(BF16), wp_ref[...], preferred_element_type=F32)
    xo = x + gate * e
    o_ref[...] = xo
    if maybe_h_ref:
        maybe_h_ref[0][...] = _rms(xo, gn_ref[...]).astype(BF16)


def _ple(x, p, g, wg, wp, g_next, tm=512):
    t, d = x.shape
    pd = p.shape[1]
    row = lambda i: (i, 0)
    full = lambda i: (0, 0)
    emit_next = g_next is not None
    out_shape = [jax.ShapeDtypeStruct((t, d), F32)]
    out_specs = [pl.BlockSpec((tm, d), row)]
    if emit_next:
        out_shape.append(jax.ShapeDtypeStruct((t, d), BF16))
        out_specs.append(pl.BlockSpec((tm, d), row))
    gn = (g_next if emit_next else g).reshape(1, d)
    res = pl.pallas_call(
        _ple_kernel,
        grid=(t // tm,),
        in_specs=[pl.BlockSpec((tm, d), row), pl.BlockSpec((tm, pd), row),
                  pl.BlockSpec((1, d), full), pl.BlockSpec((d, d), full),
                  pl.BlockSpec((pd, d), full), pl.BlockSpec((1, d), full)],
        out_specs=out_specs,
        out_shape=out_shape,
        compiler_params=_cparams("parallel"),
        name="ple_gate",
    )(x, p, g.reshape(1, d), wg, wp, gn)
    return (res[0], res[1]) if emit_next else (res[0], None)


def kernel(x, p, norm_mix, w_in, gla_gate_w2, gla_gate_b, gla_out_norm, moba_q_norm,
           moba_k_norm, w_branch_a, w_branch_b, w_out, norm_mlp, w_up, w_down,
           norm_ple, w_ple_gate, w_ple):
    batch, seq, d = x.shape
    depth = w_in.shape[0]
    t = batch * seq
    x = x.reshape(t, d)

    n_gqk = 2 * GLA_HEADS * GLA_DK
    n_gv = GLA_HEADS * GLA_DV
    n_m = MOBA_HEADS * MOBA_DH
    o_gv = n_gqk
    o_lr = o_gv + n_gv
    o_gr = o_lr + GLA_GATE_RANK
    o_mqk = o_gr + n_gv
    o_mv = o_mqk + 2 * n_m
    o_gate = o_mv + n_m

    slopes = 2.0 ** (-8.0 * jnp.arange(1, MOBA_HEADS + 1, dtype=F32) / MOBA_HEADS)
    slopes = jnp.broadcast_to(slopes[:, None, None], (MOBA_HEADS, 1, MOBA_BLOCK))
    qk_scales = jnp.stack([jnp.full((1, MOBA_DH), MOBA_DH ** -0.5, F32),
                           jnp.ones((1, MOBA_DH), F32)])

    h = _norm_cast(x, norm_mix[0])
    for li in range(depth):
        w = w_in[li]
        w_gqk = w[:, :o_gv].astype(BF16)
        w_gv = w[:, o_gv:o_lr].astype(BF16)
        w_lr = jnp.pad(w[:, o_lr:o_gr], ((0, 0), (0, LANES - GLA_GATE_RANK))).astype(BF16)
        w2 = jnp.pad(gla_gate_w2[li], ((0, LANES - GLA_GATE_RANK), (0, 0))).astype(BF16)
        w_gr = w[:, o_gr:o_mqk].astype(BF16)
        w_mqk = w[:, o_mqk:o_mv].astype(BF16)
        w_mv = w[:, o_mv:o_gate].astype(BF16)
        w_gate = w[:, o_gate:].astype(BF16)

        gqk = _proj(h, w_gqk, F32, name="gla_qk_proj")
        gv = _proj(h, w_gv, BF16, name="gla_v_proj")
        log_a = _loga_proj(h, w_lr, w2, gla_gate_b[li].reshape(1, -1))
        gr = _proj(h, w_gr, F32, act="silu", name="gla_outgate_proj")
        mqk = _qk_proj(h, w_mqk, jnp.stack([moba_q_norm[li], moba_k_norm[li]])[:, None, :],
                       qk_scales)
        mv = _proj(h, w_mv, BF16, name="moba_v_proj")
        gates = _proj(h, w_gate, F32, act="sigmoid", name="branch_gate_proj")

        oa = _gla(gqk, gv, log_a, gr, gla_out_norm[li], batch, seq)
        ob = _moba(mqk, mv, slopes, batch, seq)

        x = _merge(x, oa, ob, gates, w_branch_a[li].astype(BF16), w_branch_b[li].astype(BF16),
                   w_out[li].astype(BF16))
        x = _mlp(x, norm_mlp[li], w_up[li].astype(BF16), w_down[li].astype(BF16))
        g_next = norm_mix[li + 1] if li + 1 < depth else None
        x, h = _ple(x, p[li].reshape(t, -1), norm_ple[li], w_ple_gate[li].astype(BF16),
                    w_ple[li].astype(BF16), g_next)
    return x.reshape(batch, seq, d)
```

```python
import functools

import jax
import jax.numpy as jnp
from jax import lax
from jax.experimental import pallas as pl
from jax.experimental.pallas import tpu as pltpu

F32 = jnp.float32
BF16 = jnp.bfloat16

EPS = 1e-6
GLA_HEADS = 4
GLA_DK = 128
GLA_DV = 256
GLA_GATE_RANK = 16
GLA_GATE_TAU = 16.0
GLA_CHUNK = 64
GLA_ROW_BLOCK = 256
MOBA_HEADS = 8
MOBA_DH = 128
MOBA_BLOCK = 256
MOBA_TOPK = 3
MOBA_HEADS_PER_STEP = 4

LANES = 128
VMEM_LIMIT = 48 * 1024 * 1024
NEG_BIG = -1e30

NT_DIMS = (((1,), (1,)), ((), ()))
TN_DIMS = (((0,), (0,)), ((), ()))


def _cparams(*sem):
    return pltpu.CompilerParams(dimension_semantics=sem, vmem_limit_bytes=VMEM_LIMIT)


def _rms(x, g):
    return x * lax.rsqrt(jnp.mean(x * x, axis=-1, keepdims=True) + EPS) * g


def _sigmoid(x):
    return 1.0 / (1.0 + jnp.exp(-x))


def _norm_kernel(x_ref, g_ref, o_ref):
    o_ref[...] = _rms(x_ref[...], g_ref[...]).astype(o_ref.dtype)


def _norm_cast(x, g, tm=1024):
    t, d = x.shape
    return pl.pallas_call(
        _norm_kernel,
        grid=(t // tm,),
        in_specs=[pl.BlockSpec((tm, d), lambda i: (i, 0)),
                  pl.BlockSpec((1, d), lambda i: (0, 0))],
        out_specs=pl.BlockSpec((tm, d), lambda i: (i, 0)),
        out_shape=jax.ShapeDtypeStruct((t, d), BF16),
        compiler_params=_cparams("parallel"),
        name="norm_cast",
    )(x, g.reshape(1, d))


def _proj_kernel(h_ref, w_ref, o_ref, *, act):
    acc = jnp.dot(h_ref[...], w_ref[...], preferred_element_type=F32)
    if act == "silu":
        acc = acc * _sigmoid(acc)
    elif act == "sigmoid":
        acc = _sigmoid(acc)
    o_ref[...] = acc.astype(o_ref.dtype)


def _proj(h, w, out_dtype, act=None, tm=512, tn=1024, name="proj"):
    t, d = h.shape
    n = w.shape[1]
    return pl.pallas_call(
        functools.partial(_proj_kernel, act=act),
        grid=(n // tn, t // tm),
        in_specs=[pl.BlockSpec((tm, d), lambda j, i: (i, 0)),
                  pl.BlockSpec((d, tn), lambda j, i: (0, j))],
        out_specs=pl.BlockSpec((tm, tn), lambda j, i: (i, j)),
        out_shape=jax.ShapeDtypeStruct((t, n), out_dtype),
        compiler_params=_cparams("parallel", "parallel"),
        name=name,
    )(h, w)


def _k_proj_kernel(h_ref, w_ref, g_ref, o_ref):
    acc = jnp.dot(h_ref[...], w_ref[...], preferred_element_type=F32)
    g = g_ref[...]
    for hh in range(acc.shape[1] // MOBA_DH):
        seg = acc[:, hh * MOBA_DH:(hh + 1) * MOBA_DH]
        o_ref[:, hh * MOBA_DH:(hh + 1) * MOBA_DH] = _rms(seg, g).astype(o_ref.dtype)


def _k_proj(h, w, gain, tm=512):
    t, d = h.shape
    n = w.shape[1]
    return pl.pallas_call(
        _k_proj_kernel,
        grid=(t // tm,),
        in_specs=[pl.BlockSpec((tm, d), lambda i: (i, 0)),
                  pl.BlockSpec((d, n), lambda i: (0, 0)),
                  pl.BlockSpec((1, MOBA_DH), lambda i: (0, 0))],
        out_specs=pl.BlockSpec((tm, n), lambda i: (i, 0)),
        out_shape=jax.ShapeDtypeStruct((t, n), BF16),
        compiler_params=_cparams("parallel"),
        name="moba_k_proj",
    )(h, w, gain.reshape(1, MOBA_DH))


def _proj_t_kernel(h_ref, wt_ref, *rest, scale):
    o_ref = rest[-1]
    blk = o_ref.shape[2]
    acc = lax.dot_general(wt_ref[...], h_ref[...], NT_DIMS, preferred_element_type=F32)
    if len(rest) == 2:
        g = rest[0][...] * scale
        segs = []
        for hh in range(acc.shape[0] // MOBA_DH):
            seg = acc[hh * MOBA_DH:(hh + 1) * MOBA_DH, :]
            ms = jnp.mean(seg * seg, axis=0, keepdims=True)
            segs.append(seg * lax.rsqrt(ms + EPS) * g)
        acc = jnp.concatenate(segs, axis=0)
    for c in range(o_ref.shape[0]):
        o_ref[c] = acc[:, c * blk:(c + 1) * blk].astype(o_ref.dtype)


def _proj_t(h, wt, gain=None, scale=1.0, tm=512, blk=MOBA_BLOCK, name="proj_t"):
    t, d = h.shape
    n = wt.shape[0]
    in_specs = [pl.BlockSpec((tm, d), lambda i: (i, 0)),
                pl.BlockSpec((n, d), lambda i: (0, 0))]
    args = [h, wt]
    if gain is not None:
        in_specs.append(pl.BlockSpec((MOBA_DH, 1), lambda i: (0, 0)))
        args.append(gain.reshape(MOBA_DH, 1))
    return pl.pallas_call(
        functools.partial(_proj_t_kernel, scale=scale),
        grid=(t // tm,),
        in_specs=in_specs,
        out_specs=pl.BlockSpec((tm // blk, n, blk), lambda i: (i, 0, 0)),
        out_shape=jax.ShapeDtypeStruct((t // blk, n, blk), BF16),
        compiler_params=_cparams("parallel"),
        name=name,
    )(*args)


def _loga_kernel(h_ref, wlr_ref, w2_ref, b_ref, o_ref):
    lr = jnp.dot(h_ref[...], wlr_ref[...], preferred_element_type=F32)
    z = jnp.dot(lr.astype(BF16), w2_ref[...], preferred_element_type=F32) + b_ref[...]
    log_sig = jnp.minimum(z, 0.0) - jnp.log(1.0 + jnp.exp(-jnp.abs(z)))
    o_ref[...] = log_sig * (1.0 / GLA_GATE_TAU)


def _loga_proj(h, wlr, w2, b, tm=1024):
    t, d = h.shape
    r = wlr.shape[1]
    n = w2.shape[1]
    return pl.pallas_call(
        _loga_kernel,
        grid=(t // tm,),
        in_specs=[pl.BlockSpec((tm, d), lambda i: (i, 0)),
                  pl.BlockSpec((d, r), lambda i: (0, 0)),
                  pl.BlockSpec((r, n), lambda i: (0, 0)),
                  pl.BlockSpec((1, n), lambda i: (0, 0))],
        out_specs=pl.BlockSpec((tm, n), lambda i: (i, 0)),
        out_shape=jax.ShapeDtypeStruct((t, n), F32),
        compiler_params=_cparams("parallel"),
        name="gla_loga_proj",
    )(h, wlr, w2, b)


def _gla_kernel(q_ref, k_ref, v_ref, la_ref, gr_ref, gn_ref, o_ref, st_ref, tri_ref, up_ref, *, rb):
    c = GLA_CHUNK
    nc = rb // c
    shift = c.bit_length() - 1
    row = lax.broadcasted_iota(jnp.int32, (rb, rb), 0)
    col = lax.broadcasted_iota(jnp.int32, (rb, rb), 1)
    same_chunk = (row >> shift) == (col >> shift)
    causal = same_chunk & (row >= col)

    @pl.when(pl.program_id(1) == 0)
    def _():
        st_ref[...] = jnp.zeros_like(st_ref)
        tri_ref[...] = causal.astype(BF16)
        up_ref[...] = (same_chunk & (row < col)).astype(BF16)

    tri = tri_ref[...]
    up = up_ref[...]
    for hh in range(GLA_HEADS):
        kc = slice(hh * GLA_DK, (hh + 1) * GLA_DK)
        vc = slice(hh * GLA_DV, (hh + 1) * GLA_DV)
        la = la_ref[:, kc]
        la_hi = la.astype(BF16)
        la_lo = (la - la_hi.astype(F32)).astype(BF16)
        b = (jnp.dot(tri, la_hi, preferred_element_type=F32)
             + jnp.dot(tri, la_lo, preferred_element_type=F32))
        b_rest = (jnp.dot(up, la_hi, preferred_element_type=F32)
                  + jnp.dot(up, la_lo, preferred_element_type=F32))
        q = q_ref[:, kc]
        k = k_ref[:, kc]
        v = v_ref[:, vc]
        qd = (q * jnp.exp(b) * (GLA_DK ** -0.5)).astype(BF16)
        kd = (k * jnp.exp(-b)).astype(BF16)
        kl = (k * jnp.exp(b_rest)).astype(BF16)
        a = lax.dot_general(qd, kd, NT_DIMS, preferred_element_type=F32)
        a = jnp.where(causal, a, 0.0).astype(BF16)
        o_intra = jnp.dot(a, v, preferred_element_type=F32)
        gn = gn_ref[hh]
        st = st_ref[hh]
        for ci in range(nc):
            rs = slice(ci * c, (ci + 1) * c)
            o = o_intra[rs] + lax.dot_general(qd[rs], st.astype(BF16), NT_DIMS,
                                              preferred_element_type=F32)
            kv_t = lax.dot_general(v[rs], kl[rs], TN_DIMS, preferred_element_type=F32)
            st = st * jnp.exp(b[ci * c + c - 1:ci * c + c, :]) + kv_t
            o_ref[rs, vc] = (_rms(o, gn) * gr_ref[rs, vc]).astype(o_ref.dtype)
        st_ref[hh] = st


def _gla(gqk, gv, log_a, gr, gnorm, batch, seq, rb=GLA_ROW_BLOCK):
    t = gqk.shape[0]
    nblk = seq // rb
    h = GLA_HEADS
    row_map = lambda b, s: (b * nblk + s, 0)
    return pl.pallas_call(
        functools.partial(_gla_kernel, rb=rb),
        grid=(batch, nblk),
        in_specs=[pl.BlockSpec((rb, h * GLA_DK), row_map),
                  pl.BlockSpec((rb, h * GLA_DK), lambda b, s: (b * nblk + s, 1)),
                  pl.BlockSpec((rb, h * GLA_DV), row_map),
                  pl.BlockSpec((rb, h * GLA_DK), row_map),
                  pl.BlockSpec((rb, h * GLA_DV), row_map),
                  pl.BlockSpec((h, 1, GLA_DV), lambda b, s: (0, 0, 0))],
        out_specs=pl.BlockSpec((rb, h * GLA_DV), row_map),
        out_shape=jax.ShapeDtypeStruct((t, h * GLA_DV), BF16),
        scratch_shapes=[pltpu.VMEM((h, GLA_DV, GLA_DK), F32),
                        pltpu.VMEM((rb, rb), BF16),
                        pltpu.VMEM((rb, rb), BF16)],
        compiler_params=_cparams("parallel", "arbitrary"),
        name="gla_scan",
    )(gqk, gqk, gv, log_a, gr, gnorm.reshape(h, 1, GLA_DV))


def _moba_build_keys(k, slope, kaug_ref, kmh_ref, kml_ref, nb):
    blk, dh = MOBA_BLOCK, MOBA_DH
    c = lax.broadcasted_iota(jnp.int32, (blk, dh), 1)
    r = lax.broadcasted_iota(jnp.int32, (blk, dh), 0).astype(F32)
    sl = slope[:, :dh]
    means = []
    for n in range(nb):
        kn = k[n * blk:(n + 1) * blk, :]
        kaug_ref[n * blk:(n + 1) * blk, :dh] = kn
        e = jnp.where(c == n, 1.0, 0.0)
        e = jnp.where((c == nb) | (c == nb + 2), 1.0, e)
        e = jnp.where(c == nb + 1, sl * r, e)
        e = jnp.where(c == nb + 3, sl * float(n * blk), e)
        kaug_ref[n * blk:(n + 1) * blk, dh:] = e.astype(BF16)
        means.append(jnp.mean(kn.astype(F32), axis=0, keepdims=True))
    km = jnp.concatenate(means, axis=0)
    km_hi = km.astype(BF16)
    kmh_ref[...] = km_hi
    kml_ref[...] = (km - km_hi.astype(F32)).astype(BF16)


def _moba_aug_queries(q_t, km_hi, km_lo, slope, i, nb):
    blk, dh = MOBA_BLOCK, MOBA_DH
    g_t = (jnp.dot(km_hi, q_t, preferred_element_type=F32)
           + jnp.dot(km_lo, q_t, preferred_element_type=F32))
    nidx = lax.broadcasted_iota(jnp.int32, (nb, blk), 0)
    valid = nidx < i
    g = jnp.where(valid, g_t, -jnp.inf)
    rank = jnp.zeros((nb, blk), jnp.int32)
    for m in range(nb):
        gm = g[m:m + 1, :]
        beats = (gm > g) | ((gm == g) & (nidx > m))
        rank = rank + beats.astype(jnp.int32)
    selb = jnp.where(valid & (rank < MOBA_TOPK), 0.0, NEG_BIG)

    ridx = lax.broadcasted_iota(jnp.int32, (dh - nb, blk), 0) + nb
    t_rel = lax.broadcasted_iota(jnp.int32, (dh - nb, blk), 1).astype(F32)
    i_f = i.astype(F32)
    rest = jnp.where(ridx == nb, -slope * t_rel, 0.0)
    rest = jnp.where((ridx == nb + 1) | (ridx == nb + 3), 1.0, rest)
    rest = jnp.where(ridx == nb + 2, -slope * (i_f * float(blk)), rest)
    x_t = jnp.concatenate([selb, rest], axis=0)
    return jnp.concatenate([q_t, x_t.astype(BF16)], axis=0)


def _moba_kernel(qt_ref, k_ref, vt_ref, slope_ref, o_ref, kaug_ref, qaug_ref, kmh_ref, kml_ref,
                 s0_ref, s1_ref, acc_ref, *, nb, hp):
    blk, dh = MOBA_BLOCK, MOBA_DH
    i = pl.program_id(2)
    hs = [slice(hh * dh, (hh + 1) * dh) for hh in range(hp)]

    @pl.when(i == 0)
    def _():
        for hh in range(hp):
            _moba_build_keys(k_ref[:, hs[hh]], slope_ref[hh], kaug_ref.at[hh],
                             kmh_ref.at[hh], kml_ref.at[hh], nb)

    for hh in range(hp):
        qaug_ref[hh] = _moba_aug_queries(qt_ref[hs[hh], :], kmh_ref[hh], kml_ref[hh],
                                         slope_ref[hh], i, nb)

    def past_scores(hh, j):
        rows = pl.ds(pl.multiple_of(j * blk, blk), blk)
        return jnp.dot(kaug_ref[hh, rows, :], qaug_ref[hh], preferred_element_type=F32)

    def update(hh, s, j, m, l):
        m_new = jnp.maximum(m, jnp.max(s, axis=0, keepdims=True))
        alpha = jnp.exp(m - m_new)
        p = jnp.exp(s - m_new)
        l = alpha * l + jnp.sum(p, axis=0, keepdims=True)
        pv = jnp.dot(vt_ref[j, hs[hh], :], p.astype(BF16), preferred_element_type=F32)
        acc_ref[hh] = alpha * acc_ref[hh] + pv
        return m_new, l

    kk = lax.broadcasted_iota(jnp.int32, (blk, blk), 0)
    qq = lax.broadcasted_iota(jnp.int32, (blk, blk), 1)
    dist = (qq - kk).astype(F32)
    own_rows = pl.ds(pl.multiple_of(i * blk, blk), blk)
    carry = []
    for hh in range(hp):
        s0_ref[hh] = past_scores(hh, 0)
        s = jnp.dot(k_ref[own_rows, hs[hh]], qt_ref[hs[hh], :], preferred_element_type=F32)
        s = jnp.where(qq >= kk, s - slope_ref[hh] * dist, NEG_BIG)
        acc_ref[hh] = jnp.zeros((dh, blk), F32)
        carry.append(update(hh, s, i, jnp.full((1, blk), -jnp.inf, F32),
                            jnp.zeros((1, blk), F32)))

    def body(jj, carry):
        j0 = 2 * jj
        j2 = jnp.minimum(j0 + 2, nb - 1)
        for hh in range(hp):
            s1_ref[hh] = past_scores(hh, j0 + 1)
        carry = [update(hh, s0_ref[hh], j0, *carry[hh]) for hh in range(hp)]
        for hh in range(hp):
            s0_ref[hh] = past_scores(hh, j2)
        return tuple(update(hh, s1_ref[hh], j0 + 1, *carry[hh]) for hh in range(hp))

    final = lax.fori_loop(0, (i + 1) // 2, body, tuple(carry))
    for hh in range(hp):
        _, l = final[hh]
        o_ref[:, hs[hh]] = (acc_ref[hh] / l).T.astype(o_ref.dtype)


def _moba(q_t, k, v_t, slopes, batch, seq, hp=MOBA_HEADS_PER_STEP):
    t = k.shape[0]
    h, dh, blk = MOBA_HEADS, MOBA_DH, MOBA_BLOCK
    nb = seq // blk
    hg = h // hp
    return pl.pallas_call(
        functools.partial(_moba_kernel, nb=nb, hp=hp),
        grid=(batch, hg, nb),
        in_specs=[pl.BlockSpec((None, hp * dh, blk), lambda b, g, i: (b * nb + i, g, 0)),
                  pl.BlockSpec((seq, hp * dh), lambda b, g, i: (b, g)),
                  pl.BlockSpec((nb, hp * dh, blk), lambda b, g, i: (b, g, 0)),
                  pl.BlockSpec((hp, 1, blk), lambda b, g, i: (g, 0, 0))],
        out_specs=pl.BlockSpec((blk, hp * dh), lambda b, g, i: (b * nb + i, g)),
        out_shape=jax.ShapeDtypeStruct((t, h * dh), BF16),
        scratch_shapes=[pltpu.VMEM((hp, seq, 2 * dh), BF16),
                        pltpu.VMEM((hp, 2 * dh, blk), BF16),
                        pltpu.VMEM((hp, nb, dh), BF16),
                        pltpu.VMEM((hp, nb, dh), BF16),
                        pltpu.VMEM((hp, blk, blk), F32),
                        pltpu.VMEM((hp, blk, blk), F32),
                        pltpu.VMEM((hp, dh, blk), F32)],
        compiler_params=_cparams("parallel", "parallel", "arbitrary"),
        name="moba_attn",
    )(q_t, k, v_t, slopes)


def _merge_kernel(x_ref, oa_ref, ob_ref, gate_ref, wa_ref, wb_ref, wo_ref, o_ref):
    d = x_ref.shape[1]
    ya = jnp.dot(oa_ref[...], wa_ref[...], preferred_element_type=F32)
    yb = jnp.dot(ob_ref[...], wb_ref[...], preferred_element_type=F32)
    y = gate_ref[:, :d] * ya + gate_ref[:, d:] * yb
    o_ref[...] = x_ref[...] + jnp.dot(y.astype(BF16), wo_ref[...], preferred_element_type=F32)


def _merge(x, oa, ob, gates, wa, wb, wo, tm=512):
    t, d = x.shape
    row = lambda i: (i, 0)
    full = lambda i: (0, 0)
    return pl.pallas_call(
        _merge_kernel,
        grid=(t // tm,),
        in_specs=[pl.BlockSpec((tm, d), row), pl.BlockSpec((tm, d), row),
                  pl.BlockSpec((tm, d), row), pl.BlockSpec((tm, 2 * d), row),
                  pl.BlockSpec((d, d), full), pl.BlockSpec((d, d), full),
                  pl.BlockSpec((d, d), full)],
        out_specs=pl.BlockSpec((tm, d), row),
        out_shape=jax.ShapeDtypeStruct((t, d), F32),
        compiler_params=_cparams("parallel"),
        name="merge_out_proj",
    )(x, oa, ob, gates, wa, wb, wo)


def _mlp_kernel(x_ref, g_ref, wu_ref, wd_ref, o_ref, h_ref, acc_ref):
    f = pl.program_id(1)

    @pl.when(f == 0)
    def _():
        h_ref[...] = _rms(x_ref[...], g_ref[...]).astype(BF16)
        acc_ref[...] = jnp.zeros_like(acc_ref)

    up = jnp.dot(h_ref[...], wu_ref[...], preferred_element_type=F32)
    act = jnp.square(jnp.maximum(up, 0.0)).astype(BF16)
    acc_ref[...] += jnp.dot(act, wd_ref[...], preferred_element_type=F32)

    @pl.when(f == pl.num_programs(1) - 1)
    def _():
        o_ref[...] = x_ref[...] + acc_ref[...]


def _mlp(x, g, wu, wd, tm=512, tf=1024):
    t, d = x.shape
    ff = wu.shape[1]
    return pl.pallas_call(
        _mlp_kernel,
        grid=(t // tm, ff // tf),
        in_specs=[pl.BlockSpec((tm, d), lambda i, f: (i, 0)),
                  pl.BlockSpec((1, d), lambda i, f: (0, 0)),
                  pl.BlockSpec((d, tf), lambda i, f: (0, f)),
                  pl.BlockSpec((tf, d), lambda i, f: (f, 0))],
        out_specs=pl.BlockSpec((tm, d), lambda i, f: (i, 0)),
        out_shape=jax.ShapeDtypeStruct((t, d), F32),
        scratch_shapes=[pltpu.VMEM((tm, d), BF16), pltpu.VMEM((tm, d), F32)],
        compiler_params=_cparams("parallel", "arbitrary"),
        name="mlp_relu2",
    )(x, g.reshape(1, d), wu, wd)


def _ple_kernel(x_ref, p_ref, g_ref, wg_ref, wp_ref, gn_ref, o_ref, *maybe_h_ref):
    x = x_ref[...]
    hn = _rms(x, g_ref[...]).astype(BF16)
    gate = _sigmoid(jnp.dot(hn, wg_ref[...], preferred_element_type=F32))
    e = jnp.dot(p_ref[...].astype(BF16), wp_ref[...], preferred_element_type=F32)
    xo = x + gate * e
    o_ref[...] = xo
    if maybe_h_ref:
        maybe_h_ref[0][...] = _rms(xo, gn_ref[...]).astype(BF16)


def _ple(x, p, g, wg, wp, g_next, tm=512):
    t, d = x.shape
    pd = p.shape[1]
    row = lambda i: (i, 0)
    full = lambda i: (0, 0)
    emit_next = g_next is not None
    out_shape = [jax.ShapeDtypeStruct((t, d), F32)]
    out_specs = [pl.BlockSpec((tm, d), row)]
    if emit_next:
        out_shape.append(jax.ShapeDtypeStruct((t, d), BF16))
        out_specs.append(pl.BlockSpec((tm, d), row))
    gn = (g_next if emit_next else g).reshape(1, d)
    res = pl.pallas_call(
        _ple_kernel,
        grid=(t // tm,),
        in_specs=[pl.BlockSpec((tm, d), row), pl.BlockSpec((tm, pd), row),
                  pl.BlockSpec((1, d), full), pl.BlockSpec((d, d), full),
                  pl.BlockSpec((pd, d), full), pl.BlockSpec((1, d), full)],
        out_specs=out_specs,
        out_shape=out_shape,
        compiler_params=_cparams("parallel"),
        name="ple_gate",
    )(x, p, g.reshape(1, d), wg, wp, gn)
    return (res[0], res[1]) if emit_next else (res[0], None)


def kernel(x, p, norm_mix, w_in, gla_gate_w2, gla_gate_b, gla_out_norm, moba_q_norm,
           moba_k_norm, w_branch_a, w_branch_b, w_out, norm_mlp, w_up, w_down,
           norm_ple, w_ple_gate, w_ple):
    batch, seq, d = x.shape
    depth = w_in.shape[0]
    t = batch * seq
    x = x.reshape(t, d)

    n_gqk = 2 * GLA_HEADS * GLA_DK
    n_gv = GLA_HEADS * GLA_DV
    n_m = MOBA_HEADS * MOBA_DH
    o_gv = n_gqk
    o_lr = o_gv + n_gv
    o_gr = o_lr + GLA_GATE_RANK
    o_mqk = o_gr + n_gv
    o_mv = o_mqk + 2 * n_m
    o_gate = o_mv + n_m

    slopes = 2.0 ** (-8.0 * jnp.arange(1, MOBA_HEADS + 1, dtype=F32) / MOBA_HEADS)
    slopes = jnp.broadcast_to(slopes[:, None, None], (MOBA_HEADS, 1, MOBA_BLOCK))

    h = _norm_cast(x, norm_mix[0])
    for li in range(depth):
        w = w_in[li]
        w_gqk = w[:, :o_gv].astype(BF16)
        w_gv = w[:, o_gv:o_lr].astype(BF16)
        w_lr = jnp.pad(w[:, o_lr:o_gr], ((0, 0), (0, LANES - GLA_GATE_RANK))).astype(BF16)
        w2 = jnp.pad(gla_gate_w2[li], ((0, LANES - GLA_GATE_RANK), (0, 0))).astype(BF16)
        w_gr = w[:, o_gr:o_mqk].astype(BF16)
        w_mq_t = w[:, o_mqk:o_mqk + n_m].T.astype(BF16)
        w_mk = w[:, o_mqk + n_m:o_mv].astype(BF16)
        w_mv_t = w[:, o_mv:o_gate].T.astype(BF16)
        w_gate = w[:, o_gate:].astype(BF16)

        gqk = _proj(h, w_gqk, F32, name="gla_qk_proj")
        gv = _proj(h, w_gv, BF16, name="gla_v_proj")
        log_a = _loga_proj(h, w_lr, w2, gla_gate_b[li].reshape(1, -1))
        gr = _proj(h, w_gr, F32, act="silu", name="gla_outgate_proj")
        mq_t = _proj_t(h, w_mq_t, moba_q_norm[li], MOBA_DH ** -0.5, name="moba_q_proj")
        mk = _k_proj(h, w_mk, moba_k_norm[li])
        mv_t = _proj_t(h, w_mv_t, name="moba_v_proj")
        gates = _proj(h, w_gate, F32, act="sigmoid", name="branch_gate_proj")

        oa = _gla(gqk, gv, log_a, gr, gla_out_norm[li], batch, seq)
        ob = _moba(mq_t, mk, mv_t, slopes, batch, seq)


        x = _merge(x, oa, ob, gates, w_branch_a[li].astype(BF16), w_branch_b[li].astype(BF16),
                   w_out[li].astype(BF16))
        x = _mlp(x, norm_mlp[li], w_up[li].astype(BF16), w_down[li].astype(BF16))
        g_next = norm_mix[li + 1] if li + 1 < depth else None
        x, h = _ple(x, p[li].reshape(t, -1), norm_ple[li], w_ple_gate[li].astype(BF16),
                    w_ple[li].astype(BF16), g_next)
    return x.reshape(batch, seq, d)
```

```python
import functools

import jax
import jax.numpy as jnp
from jax import lax
from jax.experimental import pallas as pl
from jax.experimental.pallas import tpu as pltpu

F32 = jnp.float32
BF16 = jnp.bfloat16

EPS = 1e-6
GLA_HEADS = 4
GLA_DK = 128
GLA_DV = 256
GLA_GATE_RANK = 16
GLA_GATE_TAU = 16.0
GLA_CHUNK = 64
GLA_ROW_BLOCK = 256
MOBA_HEADS = 8
MOBA_DH = 128
MOBA_BLOCK = 256
MOBA_TOPK = 3
MOBA_HEADS_PER_STEP = 4

LANES = 128
VMEM_LIMIT = 48 * 1024 * 1024
NEG_BIG = -1e30

NT_DIMS = (((1,), (1,)), ((), ()))
TN_DIMS = (((0,), (0,)), ((), ()))


def _cparams(*sem):
    return pltpu.CompilerParams(dimension_semantics=sem, vmem_limit_bytes=VMEM_LIMIT)


def _rms(x, g):
    return x * lax.rsqrt(jnp.mean(x * x, axis=-1, keepdims=True) + EPS) * g


def _sigmoid(x):
    return 1.0 / (1.0 + jnp.exp(-x))


def _norm_kernel(x_ref, g_ref, o_ref):
    o_ref[...] = _rms(x_ref[...], g_ref[...]).astype(o_ref.dtype)


def _norm_cast(x, g, tm=1024):
    t, d = x.shape
    return pl.pallas_call(
        _norm_kernel,
        grid=(t // tm,),
        in_specs=[pl.BlockSpec((tm, d), lambda i: (i, 0)),
                  pl.BlockSpec((1, d), lambda i: (0, 0))],
        out_specs=pl.BlockSpec((tm, d), lambda i: (i, 0)),
        out_shape=jax.ShapeDtypeStruct((t, d), BF16),
        compiler_params=_cparams("parallel"),
        name="norm_cast",
    )(x, g.reshape(1, d))


def _moba_proj_kernel(h_ref, wqt_ref, wk_ref, wvt_ref, gq_ref, gk_ref, qt_ref, k_ref, vt_ref,
                      *, scale):
    dh = MOBA_DH
    blk = qt_ref.shape[2]
    h = h_ref[...]
    nheads = k_ref.shape[1] // dh

    q_t = lax.dot_general(wqt_ref[...], h, NT_DIMS, preferred_element_type=F32)
    gq = gq_ref[...] * scale
    segs = []
    for hh in range(nheads):
        seg = q_t[hh * dh:(hh + 1) * dh, :]
        ms = jnp.mean(seg * seg, axis=0, keepdims=True)
        segs.append((seg * lax.rsqrt(ms + EPS) * gq).astype(qt_ref.dtype))
    q_t = jnp.concatenate(segs, axis=0)

    k = jnp.dot(h, wk_ref[...], preferred_element_type=F32)
    gk = gk_ref[...]
    for hh in range(nheads):
        k_ref[:, hh * dh:(hh + 1) * dh] = _rms(k[:, hh * dh:(hh + 1) * dh], gk).astype(k_ref.dtype)

    v_t = lax.dot_general(wvt_ref[...], h, NT_DIMS, preferred_element_type=F32)
    for c in range(qt_ref.shape[0]):
        qt_ref[c] = q_t[:, c * blk:(c + 1) * blk]
        vt_ref[c] = v_t[:, c * blk:(c + 1) * blk].astype(vt_ref.dtype)


def _moba_proj(h, wq_t, wk, wv_t, gq, gk, tm=512, blk=MOBA_BLOCK):
    t, d = h.shape
    n = wk.shape[1]
    full = lambda i: (0, 0)
    resident = lambda shape: pl.BlockSpec(shape, full, pipeline_mode=pl.Buffered(1))
    t_spec = pl.BlockSpec((tm // blk, n, blk), lambda i: (i, 0, 0))
    t_shape = jax.ShapeDtypeStruct((t // blk, n, blk), BF16)
    return pl.pallas_call(
        functools.partial(_moba_proj_kernel, scale=MOBA_DH ** -0.5),
        grid=(t // tm,),
        in_specs=[pl.BlockSpec((tm, d), lambda i: (i, 0)),
                  resident((n, d)), resident((d, n)), resident((n, d)),
                  resident((MOBA_DH, 1)), resident((1, MOBA_DH))],
        out_specs=[t_spec, pl.BlockSpec((tm, n), lambda i: (i, 0)), t_spec],
        out_shape=[t_shape, jax.ShapeDtypeStruct((t, n), BF16), t_shape],
        compiler_params=_cparams("parallel"),
        name="moba_qkv_proj",
    )(h, wq_t, wk, wv_t, gq.reshape(MOBA_DH, 1), gk.reshape(1, MOBA_DH))


def _gla_kernel(h_ref, wqk_ref, wv_ref, wr_ref, wlr_ref, w2_ref, b_ref, gn_ref, o_ref,
                st_ref, tri_ref, up_ref, *, rb):
    n_k = GLA_HEADS * GLA_DK
    h = h_ref[...]
    qk_all = jnp.dot(h, wqk_ref[...], preferred_element_type=F32)
    v_all = jnp.dot(h, wv_ref[...], preferred_element_type=F32).astype(BF16)
    g_r = jnp.dot(h, wr_ref[...], preferred_element_type=F32)
    gr_all = g_r * _sigmoid(g_r)
    lr = jnp.dot(h, wlr_ref[...], preferred_element_type=F32)
    z = jnp.dot(lr.astype(BF16), w2_ref[...], preferred_element_type=F32) + b_ref[...]
    la_all = (jnp.minimum(z, 0.0) - jnp.log(1.0 + jnp.exp(-jnp.abs(z)))) * (1.0 / GLA_GATE_TAU)

    c = GLA_CHUNK
    nc = rb // c
    shift = c.bit_length() - 1
    row = lax.broadcasted_iota(jnp.int32, (rb, rb), 0)
    col = lax.broadcasted_iota(jnp.int32, (rb, rb), 1)
    same_chunk = (row >> shift) == (col >> shift)
    causal = same_chunk & (row >= col)

    @pl.when(pl.program_id(1) == 0)
    def _():
        st_ref[...] = jnp.zeros_like(st_ref)
        tri_ref[...] = causal.astype(BF16)
        up_ref[...] = (same_chunk & (row < col)).astype(BF16)

    tri = tri_ref[...]
    up = up_ref[...]
    for hh in range(GLA_HEADS):
        kc = slice(hh * GLA_DK, (hh + 1) * GLA_DK)
        vc = slice(hh * GLA_DV, (hh + 1) * GLA_DV)
        la = la_all[:, kc]
        la_hi = la.astype(BF16)
        la_lo = (la - la_hi.astype(F32)).astype(BF16)
        b = (jnp.dot(tri, la_hi, preferred_element_type=F32)
             + jnp.dot(tri, la_lo, preferred_element_type=F32))
        b_rest = (jnp.dot(up, la_hi, preferred_element_type=F32)
                  + jnp.dot(up, la_lo, preferred_element_type=F32))
        q = qk_all[:, kc]
        k = qk_all[:, n_k + hh * GLA_DK:n_k + (hh + 1) * GLA_DK]
        v = v_all[:, vc]
        qd =(q * jnp.exp(b) * (GLA_DK ** -0.5)).astype(BF16)
        kd = (k * jnp.exp(-b)).astype(BF16)
        kl = (k * jnp.exp(b_rest)).astype(BF16)
        a = lax.dot_general(qd, kd, NT_DIMS, preferred_element_type=F32)
        a = jnp.where(causal, a, 0.0).astype(BF16)
        o_intra = jnp.dot(a, v, preferred_element_type=F32)
        gn = gn_ref[hh]
        st = st_ref[hh]
        for ci in range(nc):
            rs = slice(ci * c, (ci + 1) * c)
            o = o_intra[rs] + lax.dot_general(qd[rs], st.astype(BF16), NT_DIMS,
                                              preferred_element_type=F32)
            kv_t = lax.dot_general(v[rs], kl[rs], TN_DIMS, preferred_element_type=F32)
            st = st * jnp.exp(b[ci * c + c - 1:ci * c + c, :]) + kv_t
            o_ref[rs, vc] = (_rms(o, gn) * gr_all[rs, vc]).astype(o_ref.dtype)
        st_ref[hh] = st


def _gla(h, w_qk, w_v, w_r, w_lr, w2, b, gnorm, batch, seq, rb=GLA_ROW_BLOCK):
    t, d = h.shape
    nblk = seq // rb
    nh = GLA_HEADS
    full = lambda bi, s: (0, 0)
    return pl.pallas_call(
        functools.partial(_gla_kernel, rb=rb),
        grid=(batch, nblk),
        in_specs=[pl.BlockSpec((rb, d), lambda bi, s: (bi * nblk + s, 0)),
                  pl.BlockSpec(w_qk.shape, full), pl.BlockSpec(w_v.shape, full),
                  pl.BlockSpec(w_r.shape, full), pl.BlockSpec(w_lr.shape, full),
                  pl.BlockSpec(w2.shape, full), pl.BlockSpec(b.shape, full),
                  pl.BlockSpec((nh, 1, GLA_DV), lambda bi, s: (0, 0, 0))],
        out_specs=pl.BlockSpec((rb, nh * GLA_DV), lambda bi, s: (bi * nblk + s, 0)),
        out_shape=jax.ShapeDtypeStruct((t, nh * GLA_DV), BF16),
        scratch_shapes=[pltpu.VMEM((nh, GLA_DV, GLA_DK), F32),
                        pltpu.VMEM((rb, rb), BF16),
                        pltpu.VMEM((rb, rb), BF16)],
        compiler_params=_cparams("parallel", "arbitrary"),
        name="gla_branch",
    )(h, w_qk, w_v, w_r, w_lr, w2, b, gnorm.reshape(nh, 1, GLA_DV))


def _moba_build_keys(k, slope, kaug_ref, kmh_ref, kml_ref, nb):
    blk, dh = MOBA_BLOCK, MOBA_DH
    c = lax.broadcasted_iota(jnp.int32, (blk, dh), 1)
    r = lax.broadcasted_iota(jnp.int32, (blk, dh), 0).astype(F32)
    sl = slope[:, :dh]
    means = []
    for n in range(nb):
        kn = k[n * blk:(n + 1) * blk, :]
        kaug_ref[n * blk:(n + 1) * blk, :dh] = kn
        e = jnp.where(c == n, 1.0, 0.0)
        e = jnp.where((c == nb) | (c == nb + 2), 1.0, e)
        e = jnp.where(c == nb + 1, sl * r, e)
        e = jnp.where(c == nb + 3, sl * float(n * blk), e)
        kaug_ref[n * blk:(n + 1) * blk, dh:] = e.astype(BF16)
        means.append(jnp.mean(kn.astype(F32), axis=0, keepdims=True))
    km = jnp.concatenate(means, axis=0)
    km_hi = km.astype(BF16)
    kmh_ref[...] = km_hi
    kml_ref[...] = (km - km_hi.astype(F32)).astype(BF16)


def _moba_aug_queries(q_t, km_hi, km_lo, slope, i, nb):
    blk, dh = MOBA_BLOCK, MOBA_DH
    g_t = (jnp.dot(km_hi, q_t, preferred_element_type=F32)
           + jnp.dot(km_lo, q_t, preferred_element_type=F32))
    nidx = lax.broadcasted_iota(jnp.int32, (nb, blk), 0)
    valid = nidx < i
    g = jnp.where(valid, g_t, -jnp.inf)
    rank = jnp.zeros((nb, blk), jnp.int32)
    for m in range(nb):
        gm = g[m:m + 1, :]
        beats = (gm > g) | ((gm == g) & (nidx > m))
        rank = rank + beats.astype(jnp.int32)
    selb = jnp.where(valid & (rank < MOBA_TOPK), 0.0, NEG_BIG)

    ridx = lax.broadcasted_iota(jnp.int32, (dh - nb, blk), 0) + nb
    t_rel = lax.broadcasted_iota(jnp.int32, (dh - nb, blk), 1).astype(F32)
    i_f = i.astype(F32)
    rest = jnp.where(ridx == nb, -slope * t_rel, 0.0)
    rest = jnp.where((ridx == nb + 1) | (ridx == nb + 3), 1.0, rest)
    rest = jnp.where(ridx == nb + 2, -slope * (i_f * float(blk)), rest)
    x_t = jnp.concatenate([selb, rest], axis=0)
    return jnp.concatenate([q_t, x_t.astype(BF16)], axis=0)


def _moba_kernel(qt_ref, k_ref, vt_ref, slope_ref, o_ref, kaug_ref, qaug_ref, kmh_ref, kml_ref,
                 s0_ref, s1_ref, acc_ref, *, nb, hp):
    blk, dh = MOBA_BLOCK, MOBA_DH
    i = pl.program_id(2)
    hs = [slice(hh * dh, (hh + 1) * dh) for hh in range(hp)]

    @pl.when(i == 0)
    def _():
        for hh in range(hp):
            _moba_build_keys(k_ref[:, hs[hh]], slope_ref[hh], kaug_ref.at[hh],
                             kmh_ref.at[hh], kml_ref.at[hh], nb)

    for hh in range(hp):
        qaug_ref[hh] = _moba_aug_queries(qt_ref[hs[hh], :], kmh_ref[hh], kml_ref[hh],
                                         slope_ref[hh], i, nb)

    def past_scores(hh, j):
        rows = pl.ds(pl.multiple_of(j * blk, blk), blk)
        return jnp.dot(kaug_ref[hh, rows, :], qaug_ref[hh], preferred_element_type=F32)

    def update(hh, s, j, m, l):
        m_new = jnp.maximum(m, jnp.max(s, axis=0, keepdims=True))
        alpha = jnp.exp(m - m_new)
        p = jnp.exp(s - m_new)
        l = alpha * l + jnp.sum(p, axis=0, keepdims=True)
        pv = jnp.dot(vt_ref[j, hs[hh], :], p.astype(BF16), preferred_element_type=F32)
        acc_ref[hh] = alpha * acc_ref[hh] + pv
        return m_new, l

    kk = lax.broadcasted_iota(jnp.int32, (blk, blk), 0)
    qq = lax.broadcasted_iota(jnp.int32, (blk, blk), 1)
    dist = (qq - kk).astype(F32)
    own_rows = pl.ds(pl.multiple_of(i * blk, blk), blk)
    carry = []
    for hh in range(hp):
        s0_ref[hh] = past_scores(hh, 0)
        s = jnp.dot(k_ref[own_rows, hs[hh]], qt_ref[hs[hh], :], preferred_element_type=F32)
        s = jnp.where(qq >= kk, s - slope_ref[hh] * dist, NEG_BIG)
        acc_ref[hh] = jnp.zeros((dh, blk), F32)
        carry.append(update(hh, s, i, jnp.full((1, blk), -jnp.inf, F32),
                            jnp.zeros((1, blk), F32)))

    def body(jj, carry):
        j0 = 2 * jj
        j2 = jnp.minimum(j0 + 2, nb - 1)
        for hh in range(hp):
            s1_ref[hh] = past_scores(hh, j0 + 1)
        carry = [update(hh, s0_ref[hh], j0, *carry[hh]) for hh in range(hp)]
        for hh in range(hp):
            s0_ref[hh] = past_scores(hh, j2)
        return tuple(update(hh, s1_ref[hh], j0 + 1, *carry[hh]) for hh in range(hp))

    final = lax.fori_loop(0, (i + 1) // 2, body, tuple(carry))
    for hh in range(hp):
        _, l = final[hh]
        o_ref[:, hs[hh]] = (acc_ref[hh] / l).T.astype(o_ref.dtype)


def _moba(q_t, k, v_t, slopes, batch, seq, hp=MOBA_HEADS_PER_STEP):
    t = k.shape[0]
    h, dh, blk = MOBA_HEADS, MOBA_DH, MOBA_BLOCK
    nb = seq // blk
    hg = h // hp
    return pl.pallas_call(
        functools.partial(_moba_kernel, nb=nb, hp=hp),
        grid=(batch, hg, nb),
        in_specs=[pl.BlockSpec((None, hp * dh, blk), lambda b, g, i: (b * nb + i, g, 0)),
                  pl.BlockSpec((seq, hp * dh), lambda b, g, i: (b, g)),
                  pl.BlockSpec((nb, hp * dh, blk), lambda b, g, i: (b, g, 0)),
                  pl.BlockSpec((hp, 1, blk), lambda b, g, i: (g, 0, 0))],
        out_specs=pl.BlockSpec((blk, hp * dh), lambda b, g, i: (b * nb + i, g)),
        out_shape=jax.ShapeDtypeStruct((t, h * dh), BF16),
        scratch_shapes=[pltpu.VMEM((hp, seq, 2 * dh), BF16),
                        pltpu.VMEM((hp, 2 * dh, blk), BF16),
                        pltpu.VMEM((hp, nb, dh), BF16),
                        pltpu.VMEM((hp, nb, dh), BF16),
                        pltpu.VMEM((hp, blk, blk), F32),
                        pltpu.VMEM((hp, blk, blk), F32),
                        pltpu.VMEM((hp, dh, blk), F32)],
        compiler_params=_cparams("parallel", "parallel", "arbitrary"),
        name="moba_attn",
    )(q_t, k, v_t, slopes)


def _merge_kernel(x_ref, h_ref, oa_ref, ob_ref, wg_ref, wa_ref, wb_ref, wo_ref, o_ref):
    d = x_ref.shape[1]
    gates = _sigmoid(jnp.dot(h_ref[...], wg_ref[...], preferred_element_type=F32))
    ya = jnp.dot(oa_ref[...], wa_ref[...], preferred_element_type=F32)
    yb = jnp.dot(ob_ref[...], wb_ref[...], preferred_element_type=F32)
    y = gates[:, :d] * ya + gates[:, d:] * yb
    o_ref[...] = x_ref[...] + jnp.dot(y.astype(BF16), wo_ref[...], preferred_element_type=F32)


def _merge(x, h, oa, ob, wg, wa, wb, wo, tm=512):
    t, d = x.shape
    row = lambda i: (i, 0)
    full = lambda i: (0, 0)
    return pl.pallas_call(
        _merge_kernel,
        grid=(t // tm,),
        in_specs=[pl.BlockSpec((tm, d), row), pl.BlockSpec((tm, d), row),
                  pl.BlockSpec((tm, d), row), pl.BlockSpec((tm, d), row),
                  pl.BlockSpec((d, 2 * d), full), pl.BlockSpec((d, d), full),
                  pl.BlockSpec((d, d), full), pl.BlockSpec((d, d), full)],
        out_specs=pl.BlockSpec((tm, d), row),
        out_shape=jax.ShapeDtypeStruct((t, d), F32),
        compiler_params=_cparams("parallel"),
        name="merge_out_proj",
    )(x, h, oa, ob, wg, wa, wb, wo)


def _mlp_ple_kernel(x_ref, p_ref, gm_ref, wu_ref, wd_ref, gp_ref, wg_ref, wp_ref, gn_ref,
                    o_ref, *maybe_h_ref, nchunk):
    x = x_ref[...]
    h2 = _rms(x, gm_ref[...]).astype(BF16)
    tf = wu_ref.shape[1] // nchunk
    acc = x
    for c in range(nchunk):
        up = jnp.dot(h2, wu_ref[:, c * tf:(c + 1) * tf], preferred_element_type=F32)
        act = jnp.square(jnp.maximum(up, 0.0)).astype(BF16)
        acc = acc + jnp.dot(act, wd_ref[c * tf:(c + 1) * tf, :], preferred_element_type=F32)
    hn = _rms(acc, gp_ref[...]).astype(BF16)
    gate = _sigmoid(jnp.dot(hn, wg_ref[...], preferred_element_type=F32))
    e = jnp.dot(p_ref[...].astype(BF16), wp_ref[...], preferred_element_type=F32)
    xo = acc + gate * e
    o_ref[...] = xo
    if maybe_h_ref:
        maybe_h_ref[0][...] = _rms(xo, gn_ref[...]).astype(BF16)


def _mlp_ple(x, p, g_mlp, wu, wd, g_ple, wg, wp, g_next, tm=512, nchunk=4):
    t, d = x.shape
    pd = p.shape[1]
    row = lambda i: (i, 0)
    full = lambda i: (0, 0)
    resident = lambda shape: pl.BlockSpec(shape, full, pipeline_mode=pl.Buffered(1))
    emit_next = g_next is not None
    out_shape = [jax.ShapeDtypeStruct((t, d), F32)]
    out_specs = [pl.BlockSpec((tm, d), row)]
    if emit_next:
        out_shape.append(jax.ShapeDtypeStruct((t, d), BF16))
        out_specs.append(pl.BlockSpec((tm, d), row))
    gn = (g_next if emit_next else g_ple).reshape(1, d)
    res = pl.pallas_call(
        functools.partial(_mlp_ple_kernel, nchunk=nchunk),
        grid=(t // tm,),
        in_specs=[pl.BlockSpec((tm, d), row), pl.BlockSpec((tm, pd), row),
                  resident((1, d)), resident(wu.shape), resident(wd.shape),
                  resident((1, d)), resident(wg.shape), resident(wp.shape), resident((1, d))],
        out_specs=out_specs,
        out_shape=out_shape,
        compiler_params=_cparams("parallel"),
        name="mlp_ple",
    )(x, p, g_mlp.reshape(1, d), wu, wd, g_ple.reshape(1, d), wg, wp, gn)
    return (res[0], res[1]) if emit_next else (res[0], None)


def kernel(x, p, norm_mix, w_in, gla_gate_w2, gla_gate_b, gla_out_norm, moba_q_norm,
           moba_k_norm, w_branch_a, w_branch_b, w_out, norm_mlp, w_up, w_down,
           norm_ple, w_ple_gate, w_ple):
    batch, seq, d = x.shape
    depth = w_in.shape[0]
    t = batch * seq
    x = x.reshape(t, d)

    n_gqk = 2 * GLA_HEADS * GLA_DK
    n_gv = GLA_HEADS * GLA_DV
    n_m = MOBA_HEADS * MOBA_DH
    o_gv = n_gqk
    o_lr = o_gv + n_gv
    o_gr = o_lr + GLA_GATE_RANK
    o_mqk = o_gr + n_gv
    o_mv = o_mqk + 2 * n_m
    o_gate = o_mv + n_m

    slopes = 2.0 ** (-8.0 * jnp.arange(1, MOBA_HEADS + 1, dtype=F32) / MOBA_HEADS)
    slopes = jnp.broadcast_to(slopes[:, None, None], (MOBA_HEADS, 1, MOBA_BLOCK))

    h = _norm_cast(x, norm_mix[0])
    for li in range(depth):
        w = w_in[li]
        w_gqk = w[:, :o_gv].astype(BF16)
        w_gv = w[:, o_gv:o_lr].astype(BF16)
        w_lr = jnp.pad(w[:, o_lr:o_gr], ((0, 0), (0, LANES - GLA_GATE_RANK))).astype(BF16)
        w2 = jnp.pad(gla_gate_w2[li], ((0, LANES - GLA_GATE_RANK), (0, 0))).astype(BF16)
        w_gr = w[:, o_gr:o_mqk].astype(BF16)
        w_mq_t = w[:, o_mqk:o_mqk + n_m].T.astype(BF16)
        w_mk = w[:, o_mqk + n_m:o_mv].astype(BF16)
        w_mv_t = w[:, o_mv:o_gate].T.astype(BF16)
        w_gate = w[:, o_gate:].astype(BF16)

        mq_t, mk, mv_t = _moba_proj(h, w_mq_t, w_mk, w_mv_t, moba_q_norm[li], moba_k_norm[li])

        oa = _gla(h, w_gqk, w_gv, w_gr, w_lr, w2, gla_gate_b[li].reshape(1, -1),
                  gla_out_norm[li], batch, seq)
        ob = _moba(mq_t, mk, mv_t, slopes, batch, seq)

        x = _merge(x, h, oa, ob, w_gate, w_branch_a[li].astype(BF16),
                   w_branch_b[li].astype(BF16), w_out[li].astype(BF16))
        g_next = norm_mix[li + 1] if li + 1 < depth else None
        x, h = _mlp_ple(x, p[li].reshape(t, -1), norm_mlp[li], w_up[li].astype(BF16),
                        w_down[li].astype(BF16), norm_ple[li], w_ple_gate[li].astype(BF16),
                        w_ple[li].astype(BF16), g_next)
    return x.reshape(batch, seq, d)
```

```python
import functools

import jax
import jax.numpy as jnp
from jax import lax
from jax.experimental import pallas as pl
from jax.experimental.pallas import tpu as pltpu

F32 = jnp.float32
BF16 = jnp.bfloat16

EPS = 1e-6
GLA_HEADS = 4
GLA_DK = 128
GLA_DV = 256
GLA_GATE_RANK = 16
GLA_GATE_TAU = 16.0
GLA_CHUNK = 64
GLA_ROW_BLOCK = 256
MOBA_HEADS = 8
MOBA_DH = 128
MOBA_BLOCK = 256
MOBA_TOPK = 3
MOBA_HEADS_PER_STEP = 4
MOBA_ONES_ROWS = 16
MOBA_DV_AUG = MOBA_DH + MOBA_ONES_ROWS
LOG2E = 1.4426950408889634

LANES = 128
VMEM_LIMIT = 48 * 1024 * 1024
NEG_BIG = -1e30

NT_DIMS = (((1,), (1,)), ((), ()))
TN_DIMS = (((0,), (0,)), ((), ()))


def _cparams(*sem):
    return pltpu.CompilerParams(dimension_semantics=sem, vmem_limit_bytes=VMEM_LIMIT)


def _rms(x, g):
    return x * lax.rsqrt(jnp.mean(x * x, axis=-1, keepdims=True) + EPS) * g


def _sigmoid(x):
    return 1.0 / (1.0 + jnp.exp(-x))


def _norm_kernel(x_ref, g_ref, o_ref):
    o_ref[...] = _rms(x_ref[...], g_ref[...]).astype(o_ref.dtype)


def _norm_cast(x, g, tm=1024):
    t, d = x.shape
    return pl.pallas_call(
        _norm_kernel,
        grid=(t // tm,),
        in_specs=[pl.BlockSpec((tm, d), lambda i: (i, 0)),
                  pl.BlockSpec((1, d), lambda i: (0, 0))],
        out_specs=pl.BlockSpec((tm, d), lambda i: (i, 0)),
        out_shape=jax.ShapeDtypeStruct((t, d), BF16),
        compiler_params=_cparams("parallel"),
        name="norm_cast",
    )(x, g.reshape(1, d))


def _moba_proj_kernel(h_ref, wqt_ref, wk_ref, wvt_ref, gq_ref, gk_ref, qt_ref, k_ref, vt_ref,
                      *, scale):
    dh = MOBA_DH
    blk = qt_ref.shape[2]
    h = h_ref[...]
    nheads = k_ref.shape[1] // dh

    q_t = lax.dot_general(wqt_ref[...], h, NT_DIMS, preferred_element_type=F32)
    gq = gq_ref[...] * scale
    segs = []
    for hh in range(nheads):
        seg = q_t[hh * dh:(hh + 1) * dh, :]
        ms = jnp.mean(seg * seg, axis=0, keepdims=True)
        segs.append((seg * lax.rsqrt(ms + EPS) * gq).astype(qt_ref.dtype))
    q_t = jnp.concatenate(segs, axis=0)

    k = jnp.dot(h, wk_ref[...], preferred_element_type=F32)
    gk = gk_ref[...]
    for hh in range(nheads):
        k_ref[:, hh * dh:(hh + 1) * dh] = _rms(k[:, hh * dh:(hh + 1) * dh], gk).astype(k_ref.dtype)

    v_t = lax.dot_general(wvt_ref[...], h, NT_DIMS, preferred_element_type=F32)
    ones = jnp.ones((MOBA_ONES_ROWS, v_t.shape[1]), vt_ref.dtype)
    parts = []
    for hh in range(nheads):
        parts += [v_t[hh * dh:(hh + 1) * dh, :].astype(vt_ref.dtype), ones]
    v_t = jnp.concatenate(parts, axis=0)
    for c in range(qt_ref.shape[0]):
        qt_ref[c] = q_t[:, c * blk:(c + 1) * blk]
        vt_ref[c] = v_t[:, c * blk:(c + 1) * blk]


def _moba_proj(h, wq_t, wk, wv_t, gq, gk, tm=512, blk=MOBA_BLOCK):
    t, d = h.shape
    n = wk.shape[1]
    full = lambda i: (0, 0)
    resident = lambda shape: pl.BlockSpec(shape, full, pipeline_mode=pl.Buffered(1))
    nv = n // MOBA_DH * MOBA_DV_AUG
    q_spec = pl.BlockSpec((tm // blk, n, blk), lambda i: (i, 0, 0))
    v_spec = pl.BlockSpec((tm // blk, nv, blk), lambda i: (i, 0, 0))
    q_shape = jax.ShapeDtypeStruct((t // blk, n, blk), BF16)
    v_shape = jax.ShapeDtypeStruct((t // blk, nv, blk), BF16)
    return pl.pallas_call(
        functools.partial(_moba_proj_kernel, scale=MOBA_DH ** -0.5 * LOG2E),
        grid=(t // tm,),
        in_specs=[pl.BlockSpec((tm, d), lambda i: (i, 0)),
                  resident((n, d)), resident((d, n)), resident((n, d)),
                  resident((MOBA_DH, 1)), resident((1, MOBA_DH))],
        out_specs=[q_spec, pl.BlockSpec((tm, n), lambda i: (i, 0)), v_spec],
        out_shape=[q_shape, jax.ShapeDtypeStruct((t, n), BF16), v_shape],
        compiler_params=_cparams("parallel"),
        name="moba_qkv_proj",
    )(h, wq_t, wk, wv_t, gq.reshape(MOBA_DH, 1), gk.reshape(1, MOBA_DH))


def _gla_kernel(h_ref, wqk_ref, wv_ref, wr_ref, wlr_ref, w2_ref, b_ref, gn_ref, o_ref,
                st_ref, tri_ref, *, rb):
    n_k = GLA_HEADS * GLA_DK
    h = h_ref[...]
    qk_all = jnp.dot(h, wqk_ref[...], preferred_element_type=F32)
    v_all = jnp.dot(h, wv_ref[...], preferred_element_type=F32).astype(BF16)
    g_r = jnp.dot(h, wr_ref[...], preferred_element_type=F32)
    gr_all = g_r * _sigmoid(g_r)
    lr = jnp.dot(h, wlr_ref[...], preferred_element_type=F32)
    z = jnp.dot(lr.astype(BF16), w2_ref[...], preferred_element_type=F32) + b_ref[...]
    la_all = (jnp.minimum(z, 0.0) - jnp.log(1.0 + jnp.exp(-jnp.abs(z)))) * (1.0 / GLA_GATE_TAU)

    c = GLA_CHUNK
    nc = rb // c
    shift = c.bit_length() - 1
    row = lax.broadcasted_iota(jnp.int32, (rb, rb), 0)
    col = lax.broadcasted_iota(jnp.int32, (rb, rb), 1)
    same_chunk = (row >> shift) == (col >> shift)
    causal = same_chunk & (row >= col)

    @pl.when(pl.program_id(1) == 0)
    def _():
        st_ref[...] = jnp.zeros_like(st_ref)
        tri_ref[...] = causal.astype(BF16)

    tri = tri_ref[...]
    heads = range(GLA_HEADS)
    kcs = [slice(hh * GLA_DK, (hh + 1) * GLA_DK) for hh in heads]
    vcs = [slice(hh * GLA_DV, (hh + 1) * GLA_DV) for hh in heads]
    la_hi = la_all.astype(BF16)
    la_lo = (la_all - la_hi.astype(F32)).astype(BF16)
    b = (jnp.dot(tri, la_hi, preferred_element_type=F32)
         + jnp.dot(tri, la_lo, preferred_element_type=F32))
    b_last = jnp.concatenate(
        [jnp.broadcast_to(b[ci * c + c - 1:ci * c + c, :], (c, b.shape[1])) for ci in range(nc)],
        axis=0)
    q_all = qk_all[:, :n_k]
    k_all = qk_all[:, n_k:]
    decay = jnp.exp(b)
    qd = (q_all * decay * (GLA_DK ** -0.5)).astype(BF16)
    kd = (k_all * jnp.exp(-b)).astype(BF16)
    kl = (k_all * jnp.exp(b_last - b)).astype(BF16)
    a = [lax.dot_general(qd[:, kcs[hh]], kd[:, kcs[hh]], NT_DIMS, preferred_element_type=F32)
         for hh in heads]
    o_intra = [jnp.dot(jnp.where(causal, a[hh], 0.0).astype(BF16), v_all[:, vcs[hh]],
                       preferred_element_type=F32) for hh in heads]
    st = [st_ref[hh] for hh in heads]
    for ci in range(nc):
        rs = slice(ci * c, (ci + 1) * c)
        last = ci * c + c - 1
        o = [o_intra[hh][rs] + lax.dot_general(qd[rs, kcs[hh]], st[hh].astype(BF16), NT_DIMS,
                                               preferred_element_type=F32) for hh in heads]
        kv_t = [lax.dot_general(v_all[rs, vcs[hh]], kl[rs, kcs[hh]], TN_DIMS,
                                preferred_element_type=F32) for hh in heads]
        for hh in heads:
            st[hh] = st[hh] * decay[last:last + 1, kcs[hh]] + kv_t[hh]
            o_ref[rs, vcs[hh]] = (_rms(o[hh], gn_ref[hh]) * gr_all[rs, vcs[hh]]).astype(o_ref.dtype)
    for hh in heads:
        st_ref[hh] = st[hh]


def _gla(h, w_qk, w_v, w_r, w_lr, w2, b, gnorm, batch, seq, rb=GLA_ROW_BLOCK):
    t, d = h.shape
    nblk = seq // rb
    nh = GLA_HEADS
    full = lambda bi, s: (0, 0)
    return pl.pallas_call(
        functools.partial(_gla_kernel, rb=rb),
        grid=(batch, nblk),
        in_specs=[pl.BlockSpec((rb, d), lambda bi, s: (bi * nblk + s, 0)),
                  pl.BlockSpec(w_qk.shape, full), pl.BlockSpec(w_v.shape, full),
                  pl.BlockSpec(w_r.shape, full), pl.BlockSpec(w_lr.shape, full),
                  pl.BlockSpec(w2.shape, full), pl.BlockSpec(b.shape, full),
                  pl.BlockSpec((nh, 1, GLA_DV), lambda bi, s: (0, 0, 0))],
        out_specs=pl.BlockSpec((rb, nh * GLA_DV), lambda bi, s: (bi * nblk + s, 0)),
        out_shape=jax.ShapeDtypeStruct((t, nh * GLA_DV), BF16),
        scratch_shapes=[pltpu.VMEM((nh, GLA_DV, GLA_DK), F32),
                        pltpu.VMEM((rb, rb), BF16)],
        compiler_params=_cparams("parallel", "arbitrary"),
        name="gla_branch",
    )(h, w_qk, w_v, w_r, w_lr, w2, b, gnorm.reshape(nh, 1, GLA_DV))


def _alibi_split(slope):
    c = slope * LOG2E
    c_hi = c.astype(BF16).astype(F32)
    return c_hi, c - c_hi


def _moba_build_keys(k, slope, kaug_ref, kmh_ref, kml_ref, nb):
    blk, dh = MOBA_BLOCK, MOBA_DH
    col = lax.broadcasted_iota(jnp.int32, (blk, dh), 1)
    r_key = lax.broadcasted_iota(jnp.int32, (blk, dh), 0).astype(F32)
    c_hi, c_lo = _alibi_split(slope[:, :dh])
    means = []
    for n in range(nb):
        kn = k[n * blk:(n + 1) * blk, :]
        kaug_ref[n * blk:(n + 1) * blk, :dh] = kn
        e = jnp.where(col == n, 1.0, 0.0)
        e = jnp.where((col == nb) | (col == nb + 4), c_hi, e)
        e = jnp.where((col == nb + 1) | (col == nb + 5), c_lo, e)
        e = jnp.where((col == nb + 2) | (col == nb + 3), r_key, e)
        e = jnp.where((col == nb + 6) | (col == nb + 7), float(n * blk), e)
        kaug_ref[n * blk:(n + 1) * blk, dh:] = e.astype(BF16)
        means.append(jnp.mean(kn.astype(F32), axis=0, keepdims=True))
    km = jnp.concatenate(means, axis=0)
    km_hi = km.astype(BF16)
    kmh_ref[...] = km_hi
    kml_ref[...] = (km - km_hi.astype(F32)).astype(BF16)


def _moba_aug_queries(q_t, g_t, slope, i, nb):
    blk, dh = MOBA_BLOCK, MOBA_DH
    nidx = lax.broadcasted_iota(jnp.int32, (nb, blk), 0)
    valid = nidx < i
    g = jnp.where(valid, g_t, -jnp.inf)
    rank = jnp.zeros((nb, blk), jnp.int32)
    for m in range(nb):
        gm = g[m:m + 1, :]
        beats = (gm > g) | ((gm == g) & (nidx > m))
        rank = rank + beats.astype(jnp.int32)
    selb = jnp.where(valid & (rank < MOBA_TOPK), 0.0, NEG_BIG)

    ridx = lax.broadcasted_iota(jnp.int32, (dh - nb, blk), 0) + nb
    r_t = lax.broadcasted_iota(jnp.int32, (dh - nb, blk), 1).astype(F32)
    c_hi, c_lo = _alibi_split(slope)
    off = i.astype(F32) * float(blk)
    rest = jnp.where((ridx == nb) | (ridx == nb + 1), -r_t, 0.0)
    rest = jnp.where((ridx == nb + 2) | (ridx == nb + 6), c_hi, rest)
    rest = jnp.where((ridx == nb + 3) | (ridx == nb + 7), c_lo, rest)
    rest = jnp.where((ridx == nb + 4) | (ridx == nb + 5), -off, rest)
    x_t = jnp.concatenate([selb, rest], axis=0)
    return jnp.concatenate([q_t, x_t.astype(BF16)], axis=0)


def _moba_kernel(qt_ref, k_ref, vt_ref, slope_ref, o_ref, kaug_ref, qaug_ref, kmh_ref, kml_ref,
                 s0_ref, s1_ref, acc_ref, *, nb, hp):
    blk, dh, dva = MOBA_BLOCK, MOBA_DH, MOBA_DV_AUG
    i = pl.program_id(2)
    hs = [slice(hh * dh, (hh + 1) * dh) for hh in range(hp)]
    vs = [slice(hh * dva, (hh + 1) * dva) for hh in range(hp)]

    @pl.when(i == 0)
    def _():
        for hh in range(hp):
            _moba_build_keys(k_ref[:, hs[hh]], slope_ref[hh], kaug_ref.at[hh],
                             kmh_ref.at[hh], kml_ref.at[hh], nb)

    own_rows = pl.ds(pl.multiple_of(i * blk, blk), blk)
    owns = [jnp.dot(k_ref[own_rows, hs[hh]], qt_ref[hs[hh], :], preferred_element_type=F32)
            for hh in range(hp)]
    gates = [jnp.dot(kmh_ref[hh], qt_ref[hs[hh], :], preferred_element_type=F32)
             + jnp.dot(kml_ref[hh], qt_ref[hs[hh], :], preferred_element_type=F32)
             for hh in range(hp)]
    for hh in range(hp):
        qaug_ref[hh] = _moba_aug_queries(qt_ref[hs[hh], :], gates[hh], slope_ref[hh], i, nb)

    def past_scores(hh, j):
        rows = pl.ds(pl.multiple_of(j * blk, blk), blk)
        s = jnp.dot(kaug_ref[hh, rows, :], qaug_ref[hh], preferred_element_type=F32)
        return s, jnp.max(s, axis=0, keepdims=True)

    def update(hh, s, s_max, j, m):
        m_new = jnp.maximum(m, s_max)
        alpha = jnp.exp2(m - m_new)
        p = jnp.exp2((s - m_new).astype(BF16))
        pv = jnp.dot(vt_ref[j, vs[hh], :], p, preferred_element_type=F32)
        acc_ref[hh] = alpha * acc_ref[hh] + pv
        return m_new

    kk = lax.broadcasted_iota(jnp.int32, (blk, blk), 0)
    qq = lax.broadcasted_iota(jnp.int32, (blk, blk), 1)
    dist = (qq - kk).astype(F32)
    max0 = []
    for hh in range(hp):
        s, s_max = past_scores(hh, 0)
        s0_ref[hh] = s
        max0.append(s_max)
    carry = []
    for hh in range(hp):
        own = jnp.where(qq >= kk, owns[hh] - (slope_ref[hh] * LOG2E) * dist, NEG_BIG)
        acc_ref[hh] = jnp.zeros((dva, blk), F32)
        m = update(hh, own, jnp.max(own, axis=0, keepdims=True), i,
                   jnp.full((1, blk), -jnp.inf, F32))
        carry.append((m, max0[hh]))

    def body(jj, carry):
        j0 = 2 * jj
        j2 = jnp.minimum(j0 + 2, nb - 1)
        ms = [c[0] for c in carry]
        max0 = [c[1] for c in carry]
        max1 = []
        for hh in range(hp):
            s, s_max = past_scores(hh, j0 + 1)
            s1_ref[hh] = s
            max1.append(s_max)
        for hh in range(hp):
            ms[hh] = update(hh, s0_ref[hh], max0[hh], j0, ms[hh])
        for hh in range(hp):
            s, max0[hh] = past_scores(hh, j2)
            s0_ref[hh] = s
        for hh in range(hp):
            ms[hh] = update(hh, s1_ref[hh], max1[hh], j0 + 1, ms[hh])
        return tuple(zip(ms, max0))

    lax.fori_loop(0, (i + 1) // 2, body, tuple(carry))
    for hh in range(hp):
        acc = acc_ref[hh]
        o_ref[:, hs[hh]] = (acc[:dh] / acc[dh:dh + 1]).T.astype(o_ref.dtype)


def _moba(q_t, k, v_t, slopes, batch, seq, hp=MOBA_HEADS_PER_STEP):
    t = k.shape[0]
    h, dh, dva, blk = MOBA_HEADS, MOBA_DH, MOBA_DV_AUG, MOBA_BLOCK
    nb = seq // blk
    hg = h // hp
    return pl.pallas_call(
        functools.partial(_moba_kernel, nb=nb, hp=hp),
        grid=(batch, hg, nb),
        in_specs=[pl.BlockSpec((None, hp * dh, blk), lambda b, g, i: (b * nb + i, g, 0)),
                  pl.BlockSpec((seq, hp * dh), lambda b, g, i: (b, g)),
                  pl.BlockSpec((nb, hp * dva, blk), lambda b, g, i: (b, g, 0)),
                  pl.BlockSpec((hp, 1, blk), lambda b, g, i: (g, 0, 0))],
        out_specs=pl.BlockSpec((blk, hp * dh), lambda b, g, i: (b * nb + i, g)),
        out_shape=jax.ShapeDtypeStruct((t, h * dh), BF16),
        scratch_shapes=[pltpu.VMEM((hp, seq, 2 * dh), BF16),
                        pltpu.VMEM((hp, 2 * dh, blk), BF16),
                        pltpu.VMEM((hp, nb, dh), BF16),
                        pltpu.VMEM((hp, nb, dh), BF16),
                        pltpu.VMEM((hp, blk, blk), F32),
                        pltpu.VMEM((hp, blk, blk), F32),
                        pltpu.VMEM((hp, dva, blk), F32)],
        compiler_params=_cparams("parallel", "parallel", "arbitrary"),
        name="moba_attn",
    )(q_t, k, v_t, slopes)


def _merge_kernel(x_ref, h_ref, oa_ref, ob_ref, wg_ref, wa_ref, wb_ref, wo_ref, o_ref):
    d = x_ref.shape[1]
    gates = _sigmoid(jnp.dot(h_ref[...], wg_ref[...], preferred_element_type=F32))
    ya = jnp.dot(oa_ref[...], wa_ref[...], preferred_element_type=F32)
    yb = jnp.dot(ob_ref[...], wb_ref[...], preferred_element_type=F32)
    y = gates[:, :d] * ya + gates[:, d:] * yb
    o_ref[...] = x_ref[...] + jnp.dot(y.astype(BF16), wo_ref[...], preferred_element_type=F32)


def _merge(x, h, oa, ob, wg, wa, wb, wo, tm=512):
    t, d = x.shape
    row = lambda i: (i, 0)
    full = lambda i: (0, 0)
    return pl.pallas_call(
        _merge_kernel,
        grid=(t // tm,),
        in_specs=[pl.BlockSpec((tm, d), row), pl.BlockSpec((tm, d), row),
                  pl.BlockSpec((tm, d), row), pl.BlockSpec((tm, d), row),
                  pl.BlockSpec((d, 2 * d), full), pl.BlockSpec((d, d), full),
                  pl.BlockSpec((d, d), full), pl.BlockSpec((d, d), full)],
        out_specs=pl.BlockSpec((tm, d), row),
        out_shape=jax.ShapeDtypeStruct((t, d), F32),
        compiler_params=_cparams("parallel"),
        name="merge_out_proj",
    )(x, h, oa, ob, wg, wa, wb, wo)


def _mlp_ple_kernel(x_ref, p_ref, gm_ref, wu_ref, wd_ref, gp_ref, wg_ref, wp_ref, gn_ref,
                    o_ref, *maybe_h_ref, nchunk):
    x = x_ref[...]
    h2 = _rms(x, gm_ref[...]).astype(BF16)
    tf = wu_ref.shape[1] // nchunk
    acc = x
    for c in range(nchunk):
        up = jnp.dot(h2, wu_ref[:, c * tf:(c + 1) * tf], preferred_element_type=F32)
        act = jnp.square(jnp.maximum(up, 0.0)).astype(BF16)
        acc = acc + jnp.dot(act, wd_ref[c * tf:(c + 1) * tf, :], preferred_element_type=F32)
    hn = _rms(acc, gp_ref[...]).astype(BF16)
    gate = _sigmoid(jnp.dot(hn, wg_ref[...], preferred_element_type=F32))
    e = jnp.dot(p_ref[...].astype(BF16), wp_ref[...], preferred_element_type=F32)
    xo = acc + gate * e
    o_ref[...] = xo
    if maybe_h_ref:
        maybe_h_ref[0][...] = _rms(xo, gn_ref[...]).astype(BF16)


def _mlp_ple(x, p, g_mlp, wu, wd, g_ple, wg, wp, g_next, tm=512, nchunk=4):
    t, d = x.shape
    pd = p.shape[1]
    row = lambda i: (i, 0)
    full = lambda i: (0, 0)
    resident = lambda shape: pl.BlockSpec(shape, full, pipeline_mode=pl.Buffered(1))
    emit_next = g_next is not None
    out_shape = [jax.ShapeDtypeStruct((t, d), F32)]
    out_specs = [pl.BlockSpec((tm, d), row)]
    if emit_next:
        out_shape.append(jax.ShapeDtypeStruct((t, d), BF16))
        out_specs.append(pl.BlockSpec((tm, d), row))
    gn = (g_next if emit_next else g_ple).reshape(1, d)
    res = pl.pallas_call(
        functools.partial(_mlp_ple_kernel, nchunk=nchunk),
        grid=(t // tm,),
        in_specs=[pl.BlockSpec((tm, d), row), pl.BlockSpec((tm, pd), row),
                  resident((1, d)), resident(wu.shape), resident(wd.shape),
                  resident((1, d)), resident(wg.shape), resident(wp.shape), resident((1, d))],
        out_specs=out_specs,
        out_shape=out_shape,
        compiler_params=_cparams("parallel"),
        name="mlp_ple",
    )(x, p, g_mlp.reshape(1, d), wu, wd, g_ple.reshape(1, d), wg, wp, gn)
    return (res[0], res[1]) if emit_next else (res[0], None)


def kernel(x, p, norm_mix, w_in, gla_gate_w2, gla_gate_b, gla_out_norm, moba_q_norm,
           moba_k_norm, w_branch_a, w_branch_b, w_out, norm_mlp, w_up, w_down,
           norm_ple, w_ple_gate, w_ple):
    batch, seq, d = x.shape
    depth = w_in.shape[0]
    t = batch * seq
    x = x.reshape(t, d)

    n_gqk = 2 * GLA_HEADS * GLA_DK
    n_gv = GLA_HEADS * GLA_DV
    n_m = MOBA_HEADS * MOBA_DH
    o_gv = n_gqk
    o_lr = o_gv + n_gv
    o_gr = o_lr + GLA_GATE_RANK
    o_mqk = o_gr + n_gv
    o_mv = o_mqk + 2 * n_m
    o_gate = o_mv + n_m

    slopes = 2.0 ** (-8.0 * jnp.arange(1, MOBA_HEADS + 1, dtype=F32) / MOBA_HEADS)
    slopes = jnp.broadcast_to(slopes[:, None, None], (MOBA_HEADS, 1, MOBA_BLOCK))

    h = _norm_cast(x, norm_mix[0])
    for li in range(depth):
        w = w_in[li]
        w_gqk = w[:, :o_gv].astype(BF16)
        w_gv = w[:, o_gv:o_lr].astype(BF16)
        w_lr = jnp.pad(w[:, o_lr:o_gr], ((0, 0), (0, LANES - GLA_GATE_RANK))).astype(BF16)
        w2 = jnp.pad(gla_gate_w2[li], ((0, LANES - GLA_GATE_RANK), (0, 0))).astype(BF16)
        w_gr = w[:, o_gr:o_mqk].astype(BF16)
        w_mq_t = w[:, o_mqk:o_mqk + n_m].T.astype(BF16)
        w_mk = w[:, o_mqk + n_m:o_mv].astype(BF16)
        w_mv_t = w[:, o_mv:o_gate].T.astype(BF16)
        w_gate = w[:, o_gate:].astype(BF16)

        mq_t, mk, mv_t = _moba_proj(h, w_mq_t, w_mk, w_mv_t, moba_q_norm[li], moba_k_norm[li])

        oa = _gla(h, w_gqk, w_gv, w_gr, w_lr, w2, gla_gate_b[li].reshape(1, -1),
                  gla_out_norm[li], batch, seq)
        ob = _moba(mq_t, mk, mv_t, slopes, batch, seq)

        x = _merge(x, h, oa, ob, w_gate, w_branch_a[li].astype(BF16),
                   w_branch_b[li].astype(BF16), w_out[li].astype(BF16))
        g_next = norm_mix[li + 1] if li + 1 < depth else None
        x, h = _mlp_ple(x, p[li].reshape(t, -1), norm_mlp[li], w_up[li].astype(BF16),
                        w_down[li].astype(BF16), norm_ple[li], w_ple_gate[li].astype(BF16),
                        w_ple[li].astype(BF16), g_next)
    return x.reshape(batch, seq, d)
```

```python
import functools

import jax
import jax.numpy as jnp
from jax import lax
from jax.experimental import pallas as pl
from jax.experimental.pallas import tpu as pltpu

F32 = jnp.float32
BF16 = jnp.bfloat16

EPS = 1e-6
GLA_HEADS = 4
GLA_DK = 128
GLA_DV = 256
GLA_GATE_RANK = 16
GLA_GATE_TAU = 16.0
GLA_CHUNK = 64
GLA_ROW_BLOCK = 256
MOBA_HEADS = 8
MOBA_DH = 128
MOBA_BLOCK = 256
MOBA_TOPK = 3
MOBA_HEADS_PER_STEP = 4
MOBA_QBLOCKS_PER_STEP = 1
MOBA_ONES_ROWS = 16
MOBA_DV_AUG = MOBA_DH + MOBA_ONES_ROWS
LOG2E = 1.4426950408889634

LANES = 128
VMEM_LIMIT = 48 * 1024 * 1024
NEG_BIG = -1e30

NT_DIMS = (((1,), (1,)), ((), ()))
TN_DIMS = (((0,), (0,)), ((), ()))


def _cparams(*sem):
    return pltpu.CompilerParams(dimension_semantics=sem, vmem_limit_bytes=VMEM_LIMIT)


def _rms(x, g):
    return x * lax.rsqrt(jnp.mean(x * x, axis=-1, keepdims=True) + EPS) * g


def _sigmoid(x):
    return 1.0 / (1.0 + jnp.exp(-x))


def _norm_kernel(x_ref, g_ref, o_ref):
    o_ref[...] = _rms(x_ref[...], g_ref[...]).astype(o_ref.dtype)


def _norm_cast(x, g, tm=1024):
    t, d = x.shape
    return pl.pallas_call(
        _norm_kernel,
        grid=(t // tm,),
        in_specs=[pl.BlockSpec((tm, d), lambda i: (i, 0)),
                  pl.BlockSpec((1, d), lambda i: (0, 0))],
        out_specs=pl.BlockSpec((tm, d), lambda i: (i, 0)),
        out_shape=jax.ShapeDtypeStruct((t, d), BF16),
        compiler_params=_cparams("parallel"),
        name="norm_cast",
    )(x, g.reshape(1, d))


def _moba_proj_kernel(h_ref, wqt_ref, wk_ref, wvt_ref, gq_ref, gk_ref, qt_ref, k_ref, vt_ref,
                      *, scale):
    dh = MOBA_DH
    blk = qt_ref.shape[2]
    h = h_ref[...]
    nheads = k_ref.shape[1] // dh

    q_t = lax.dot_general(wqt_ref[...], h, NT_DIMS, preferred_element_type=F32)
    gq = gq_ref[...] * scale
    segs = []
    for hh in range(nheads):
        seg = q_t[hh * dh:(hh + 1) * dh, :]
        ms = jnp.mean(seg * seg, axis=0, keepdims=True)
        segs.append((seg * lax.rsqrt(ms + EPS) * gq).astype(qt_ref.dtype))
    q_t = jnp.concatenate(segs, axis=0)

    k = jnp.dot(h, wk_ref[...], preferred_element_type=F32)
    gk = gk_ref[...]
    for hh in range(nheads):
        k_ref[:, hh * dh:(hh + 1) * dh] = _rms(k[:, hh * dh:(hh + 1) * dh], gk).astype(k_ref.dtype)

    v_t = lax.dot_general(wvt_ref[...], h, NT_DIMS, preferred_element_type=F32)
    ones = jnp.ones((MOBA_ONES_ROWS, v_t.shape[1]), vt_ref.dtype)
    parts = []
    for hh in range(nheads):
        parts += [v_t[hh * dh:(hh + 1) * dh, :].astype(vt_ref.dtype), ones]
    v_t = jnp.concatenate(parts, axis=0)
    for c in range(qt_ref.shape[0]):
        qt_ref[c] = q_t[:, c * blk:(c + 1) * blk]
        vt_ref[c] = v_t[:, c * blk:(c + 1) * blk]


def _moba_proj(h, wq_t, wk, wv_t, gq, gk, tm=512, blk=MOBA_BLOCK):
    t, d = h.shape
    n = wk.shape[1]
    full = lambda i: (0, 0)
    resident = lambda shape: pl.BlockSpec(shape, full, pipeline_mode=pl.Buffered(1))
    nv = n // MOBA_DH * MOBA_DV_AUG
    q_spec = pl.BlockSpec((tm // blk, n, blk), lambda i: (i, 0, 0))
    v_spec = pl.BlockSpec((tm // blk, nv, blk), lambda i: (i, 0, 0))
    q_shape = jax.ShapeDtypeStruct((t // blk, n, blk), BF16)
    v_shape = jax.ShapeDtypeStruct((t // blk, nv, blk), BF16)
    return pl.pallas_call(
        functools.partial(_moba_proj_kernel, scale=MOBA_DH ** -0.5 * LOG2E),
        grid=(t // tm,),
        in_specs=[pl.BlockSpec((tm, d), lambda i: (i, 0)),
                  resident((n, d)), resident((d, n)), resident((n, d)),
                  resident((MOBA_DH, 1)), resident((1, MOBA_DH))],
        out_specs=[q_spec, pl.BlockSpec((tm, n), lambda i: (i, 0)), v_spec],
        out_shape=[q_shape, jax.ShapeDtypeStruct((t, n), BF16), v_shape],
        compiler_params=_cparams("parallel"),
        name="moba_qkv_proj",
    )(h, wq_t, wk, wv_t, gq.reshape(MOBA_DH, 1), gk.reshape(1, MOBA_DH))


def _gla_kernel(h_ref, wqk_ref, wv_ref, wr_ref, wlr_ref, w2_ref, b_ref, gn_ref, o_ref,
                st_ref, tri_ref, *, rb):
    n_k = GLA_HEADS * GLA_DK
    h = h_ref[...]
    lr = jnp.dot(h, wlr_ref[...], preferred_element_type=F32)
    qk_all = jnp.dot(h, wqk_ref[...], preferred_element_type=F32)
    z = jnp.dot(lr.astype(BF16), w2_ref[...], preferred_element_type=F32) + b_ref[...]
    la_all = (jnp.minimum(z, 0.0) - jnp.log(1.0 + jnp.exp(-jnp.abs(z)))) * (1.0 / GLA_GATE_TAU)

    c = GLA_CHUNK
    nc = rb // c
    shift = c.bit_length() - 1
    row = lax.broadcasted_iota(jnp.int32, (rb, rb), 0)
    col = lax.broadcasted_iota(jnp.int32, (rb, rb), 1)
    same_chunk = (row >> shift) == (col >> shift)
    causal = same_chunk & (row >= col)

    @pl.when(pl.program_id(1) == 0)
    def _():
        st_ref[...] = jnp.zeros_like(st_ref)
        tri_ref[...] = causal.astype(BF16)

    tri = tri_ref[...]
    heads = range(GLA_HEADS)
    kcs = [slice(hh * GLA_DK, (hh + 1) * GLA_DK) for hh in heads]
    vcs = [slice(hh * GLA_DV, (hh + 1) * GLA_DV) for hh in heads]
    la_hi = la_all.astype(BF16)
    la_lo = (la_all - la_hi.astype(F32)).astype(BF16)
    v_all = jnp.dot(h, wv_ref[...], preferred_element_type=F32).astype(BF16)
    b = (jnp.dot(tri, la_hi, preferred_element_type=F32)
         + jnp.dot(tri, la_lo, preferred_element_type=F32))
    g_r = jnp.dot(h, wr_ref[...], preferred_element_type=F32)
    gr_all = g_r * _sigmoid(g_r)
    b_last = jnp.concatenate(
        [jnp.broadcast_to(b[ci * c + c - 1:ci * c + c, :], (c, b.shape[1])) for ci in range(nc)],
        axis=0)
    q_all = qk_all[:, :n_k]
    k_all = qk_all[:, n_k:]
    decay = jnp.exp(b)
    qd = (q_all * decay * (GLA_DK ** -0.5)).astype(BF16)
    kd = (k_all * jnp.exp(-b)).astype(BF16)
    kl = (k_all * jnp.exp(b_last - b)).astype(BF16)
    a = [lax.dot_general(qd[:, kcs[hh]], kd[:, kcs[hh]], NT_DIMS, preferred_element_type=F32)
         for hh in heads]
    chunks = [slice(ci * c, (ci + 1) * c) for ci in range(nc)]
    kv_t = [[lax.dot_general(v_all[rs, vcs[hh]], kl[rs, kcs[hh]], TN_DIMS,
                             preferred_element_type=F32) for hh in heads] for rs in chunks]
    o_intra = [jnp.dot(jnp.where(causal, a[hh], 0.0).astype(BF16), v_all[:, vcs[hh]],
                       preferred_element_type=F32) for hh in heads]
    st = [st_ref[hh] for hh in heads]
    o_inter = []
    for ci, rs in enumerate(chunks):
        o_inter.append([lax.dot_general(qd[rs, kcs[hh]], st[hh].astype(BF16), NT_DIMS,
                                        preferred_element_type=F32) for hh in heads])
        last = ci * c + c - 1
        st = [st[hh] * decay[last:last + 1, kcs[hh]] + kv_t[ci][hh] for hh in heads]
    for hh in heads:
        st_ref[hh] = st[hh]
    for ci, rs in enumerate(chunks):
        for hh in heads:
            o = o_intra[hh][rs] + o_inter[ci][hh]
            o_ref[rs, vcs[hh]] = (_rms(o, gn_ref[hh]) * gr_all[rs, vcs[hh]]).astype(o_ref.dtype)


def _gla(h, w_qk, w_v, w_r, w_lr, w2, b, gnorm, batch, seq, rb=GLA_ROW_BLOCK):
    t, d = h.shape
    nblk = seq // rb
    nh = GLA_HEADS
    full = lambda bi, s: (0, 0)
    return pl.pallas_call(
        functools.partial(_gla_kernel, rb=rb),
        grid=(batch, nblk),
        in_specs=[pl.BlockSpec((rb, d), lambda bi, s: (bi * nblk + s, 0)),
                  pl.BlockSpec(w_qk.shape, full), pl.BlockSpec(w_v.shape, full),
                  pl.BlockSpec(w_r.shape, full), pl.BlockSpec(w_lr.shape, full),
                  pl.BlockSpec(w2.shape, full), pl.BlockSpec(b.shape, full),
                  pl.BlockSpec((nh, 1, GLA_DV), lambda bi, s: (0, 0, 0))],
        out_specs=pl.BlockSpec((rb, nh * GLA_DV), lambda bi, s: (bi * nblk + s, 0)),
        out_shape=jax.ShapeDtypeStruct((t, nh * GLA_DV), BF16),
        scratch_shapes=[pltpu.VMEM((nh, GLA_DV, GLA_DK), F32),
                        pltpu.VMEM((rb, rb), BF16)],
        compiler_params=_cparams("parallel", "arbitrary"),
        name="gla_branch",
    )(h, w_qk, w_v, w_r, w_lr, w2, b, gnorm.reshape(nh, 1, GLA_DV))


def _alibi_split(slope):
    c = slope * LOG2E
    c_hi = c.astype(BF16).astype(F32)
    return c_hi, c - c_hi


def _moba_build_keys(k, slope, kaug_ref, kmh_ref, kml_ref, nb):
    blk, dh = MOBA_BLOCK, MOBA_DH
    col = lax.broadcasted_iota(jnp.int32, (blk, dh), 1)
    r_key = lax.broadcasted_iota(jnp.int32, (blk, dh), 0).astype(F32)
    c_hi, c_lo = _alibi_split(slope[:, :dh])
    base = jnp.where((col == nb) | (col == nb + 4), c_hi, 0.0)
    base = jnp.where((col == nb + 1) | (col == nb + 5), c_lo, base)
    base = jnp.where((col == nb + 2) | (col == nb + 3), r_key, base)
    is_offset = (col == nb + 6) | (col == nb + 7)
    ones = jnp.ones((8, blk), BF16)
    sums = []
    for n in range(nb):
        kn = k[n * blk:(n + 1) * blk, :]
        kaug_ref[n * blk:(n + 1) * blk, :dh] = kn
        e = jnp.where(col == n, 1.0, jnp.where(is_offset, float(n * blk), base))
        kaug_ref[n * blk:(n + 1) * blk, dh:] = e.astype(BF16)
        sums.append(jnp.dot(ones, kn, preferred_element_type=F32)[:1])
    km = jnp.concatenate(sums, axis=0) * (1.0 / blk)
    km_hi = km.astype(BF16)
    kmh_ref[...] = km_hi
    kml_ref[...] = (km - km_hi.astype(F32)).astype(BF16)


def _moba_aug_queries(q_t, g_t, slope, first_blk, nb):
    blk, dh = MOBA_BLOCK, MOBA_DH
    qw = q_t.shape[1]
    shift = blk.bit_length() - 1
    nidx = lax.broadcasted_iota(jnp.int32, (nb, qw), 0)
    q_blk = first_blk + (lax.broadcasted_iota(jnp.int32, (nb, qw), 1) >> shift)
    valid = nidx < q_blk
    g = jnp.where(valid, g_t, -jnp.inf)
    rank = jnp.zeros((nb, qw), jnp.int32)
    for m in range(nb):
        gm = g[m:m + 1, :]
        beats = (gm > g) | ((gm == g) & (nidx > m))
        rank = rank + beats.astype(jnp.int32)
    selb = jnp.where(valid & (rank < MOBA_TOPK), 0.0, NEG_BIG)

    ridx = lax.broadcasted_iota(jnp.int32, (dh - nb, qw), 0) + nb
    lane = lax.broadcasted_iota(jnp.int32, (dh - nb, qw), 1)
    r_t = (lane & (blk - 1)).astype(F32)
    off = ((first_blk + (lane >> shift)) * blk).astype(F32)
    c_hi, c_lo = _alibi_split(slope)
    rest = jnp.where((ridx == nb) | (ridx == nb + 1), -r_t, 0.0)
    rest = jnp.where((ridx == nb + 2) | (ridx == nb + 6), c_hi, rest)
    rest = jnp.where((ridx == nb + 3) | (ridx == nb + 7), c_lo, rest)
    rest = jnp.where((ridx == nb + 4) | (ridx == nb + 5), -off, rest)
    x_t = jnp.concatenate([selb, rest], axis=0)
    return jnp.concatenate([q_t, x_t.astype(BF16)], axis=0)


def _moba_kernel(qt_ref, k_ref, vt_ref, slope_ref, o_ref, kaug_ref, qaug_ref, kmh_ref, kml_ref,
                 s0_ref, s1_ref, acc_ref, *, nb, hp, qb):
    blk, dh, dva = MOBA_BLOCK, MOBA_DH, MOBA_DV_AUG
    qw = qb * blk
    step = pl.program_id(2)
    first_blk = step * qb
    hs = [slice(hh * dh, (hh + 1) * dh) for hh in range(hp)]
    vs = [slice(hh * dva, (hh + 1) * dva) for hh in range(hp)]

    @pl.when(step == 0)
    def _():
        for hh in range(hp):
            _moba_build_keys(k_ref[:, hs[hh]], slope_ref[hh][:, :blk], kaug_ref.at[hh],
                             kmh_ref.at[hh], kml_ref.at[hh], nb)

    q_ts = [jnp.concatenate([qt_ref[c, hs[hh], :] for c in range(qb)], axis=1)
            for hh in range(hp)]
    lanes = [slice(c * blk, (c + 1) * blk) for c in range(qb)]
    blk_rows = [pl.ds(pl.multiple_of((first_blk + c) * blk, blk), blk) for c in range(qb)]
    owns = [[jnp.dot(k_ref[blk_rows[c], hs[hh]], q_ts[hh][:, lanes[c]],
                     preferred_element_type=F32) for c in range(qb)]
            for hh in range(hp)]
    gates = [jnp.dot(kmh_ref[hh], q_ts[hh], preferred_element_type=F32)
             + jnp.dot(kml_ref[hh], q_ts[hh], preferred_element_type=F32)
             for hh in range(hp)]
    for hh in range(hp):
        qaug_ref[hh] = _moba_aug_queries(q_ts[hh], gates[hh], slope_ref[hh], first_blk, nb)

    def past_scores(hh, j, qcols=slice(None)):
        rows = pl.ds(pl.multiple_of(j * blk, blk), blk)
        s = jnp.dot(kaug_ref[hh, rows, :], qaug_ref[hh, :, qcols], preferred_element_type=F32)
        return s, jnp.max(s, axis=0, keepdims=True)

    def update(hh, s, s_max, j, m, qcols=slice(None)):
        m_new = jnp.maximum(m, s_max)
        alpha = jnp.exp2(m - m_new)
        p = jnp.exp2((s - m_new).astype(BF16))
        pv = jnp.dot(vt_ref[j, vs[hh], :], p, preferred_element_type=F32)
        acc_ref[hh, :, qcols] = alpha * acc_ref[hh, :, qcols] + pv
        return m_new

    kk = lax.broadcasted_iota(jnp.int32, (blk, blk), 0)
    qq = lax.broadcasted_iota(jnp.int32, (blk, blk), 1)
    dist = (qq - kk).astype(F32)
    ms = []
    for hh in range(hp):
        acc_ref[hh] = jnp.zeros((dva, qw), F32)
        c_log2 = slope_ref[hh][:, :blk] * LOG2E
        m_blocks = []
        for c in range(qb):
            own = jnp.where(qq >= kk, owns[hh][c] - c_log2 * dist, NEG_BIG)
            m_blocks.append(update(hh, own, jnp.max(own, axis=0, keepdims=True), first_blk + c,
                                   jnp.full((1, blk), -jnp.inf, F32), lanes[c]))
        ms.append(m_blocks)
    max0 = []
    for hh in range(hp):
        s, s_max = past_scores(hh, 0)
        s0_ref[hh] = s
        max0.append(s_max)
    carry = []
    for hh in range(hp):
        for c in range(qb - 1):
            later = slice((c + 1) * blk, qw)
            s, s_max = past_scores(hh, first_blk + c, later)
            m_later = update(hh, s, s_max, first_blk + c, jnp.concatenate(ms[hh][c + 1:], axis=1),
                             later)
            ms[hh][c + 1:] = [m_later[:, n * blk:(n + 1) * blk] for n in range(qb - 1 - c)]
        carry.append((jnp.concatenate(ms[hh], axis=1), max0[hh]))

    def body(jj, carry):
        j0 = 2 * jj
        j2 = jnp.minimum(j0 + 2, nb - 1)
        ms = [c[0] for c in carry]
        max0 = [c[1] for c in carry]
        max1 = []
        for hh in range(hp):
            s, s_max = past_scores(hh, j0 + 1)
            s1_ref[hh] = s
            max1.append(s_max)
        for hh in range(hp):
            ms[hh] = update(hh, s0_ref[hh], max0[hh], j0, ms[hh])
        for hh in range(hp):
            s, max0[hh] = past_scores(hh, j2)
            s0_ref[hh] = s
        for hh in range(hp):
            ms[hh] = update(hh, s1_ref[hh], max1[hh], j0 + 1, ms[hh])
        return tuple(zip(ms, max0))

    lax.fori_loop(0, (first_blk + 1) // 2, body, tuple(carry))
    for hh in range(hp):
        acc = acc_ref[hh]
        o_ref[:, hs[hh]] = (acc[:dh] / acc[dh:dh + 1]).T.astype(o_ref.dtype)


def _moba(q_t, k, v_t, slopes, batch, seq, hp=MOBA_HEADS_PER_STEP, qb=MOBA_QBLOCKS_PER_STEP):
    t = k.shape[0]
    h, dh, dva, blk = MOBA_HEADS, MOBA_DH, MOBA_DV_AUG, MOBA_BLOCK
    nb = seq // blk
    ns = nb // qb
    qw = qb * blk
    hg = h // hp
    return pl.pallas_call(
        functools.partial(_moba_kernel, nb=nb, hp=hp, qb=qb),
        grid=(batch, hg, ns),
        in_specs=[pl.BlockSpec((qb, hp * dh, blk), lambda b, g, i: (b * ns + i, g, 0)),
                  pl.BlockSpec((seq, hp * dh), lambda b, g, i: (b, g)),
                  pl.BlockSpec((nb, hp * dva, blk), lambda b, g, i: (b, g, 0)),
                  pl.BlockSpec((hp, 1, qw), lambda b, g, i: (g, 0, 0))],
        out_specs=pl.BlockSpec((qw, hp * dh), lambda b, g, i: (b * ns + i, g)),
        out_shape=jax.ShapeDtypeStruct((t, h * dh), BF16),
        scratch_shapes=[pltpu.VMEM((hp, seq, 2 * dh), BF16),
                        pltpu.VMEM((hp, 2 * dh, qw), BF16),
                        pltpu.VMEM((hp, nb, dh), BF16),
                        pltpu.VMEM((hp, nb, dh), BF16),
                        pltpu.VMEM((hp, blk, qw), F32),
                        pltpu.VMEM((hp, blk, qw), F32),
                        pltpu.VMEM((hp, dva, qw), F32)],
        compiler_params=_cparams("parallel", "parallel", "arbitrary"),
        name="moba_attn",
    )(q_t, k, v_t, slopes)


def _merge_kernel(x_ref, h_ref, oa_ref, ob_ref, wg_ref, wa_ref, wb_ref, wo_ref, o_ref):
    d = x_ref.shape[1]
    gates = _sigmoid(jnp.dot(h_ref[...], wg_ref[...], preferred_element_type=F32))
    ya = jnp.dot(oa_ref[...], wa_ref[...], preferred_element_type=F32)
    yb = jnp.dot(ob_ref[...], wb_ref[...], preferred_element_type=F32)
    y = gates[:, :d] * ya + gates[:, d:] * yb
    o_ref[...] = x_ref[...] + jnp.dot(y.astype(BF16), wo_ref[...], preferred_element_type=F32)


def _merge(x, h, oa, ob, wg, wa, wb, wo, tm=512):
    t, d = x.shape
    row = lambda i: (i, 0)
    full = lambda i: (0, 0)
    return pl.pallas_call(
        _merge_kernel,
        grid=(t // tm,),
        in_specs=[pl.BlockSpec((tm, d), row), pl.BlockSpec((tm, d), row),
                  pl.BlockSpec((tm, d), row), pl.BlockSpec((tm, d), row),
                  pl.BlockSpec((d, 2 * d), full), pl.BlockSpec((d, d), full),
                  pl.BlockSpec((d, d), full), pl.BlockSpec((d, d), full)],
        out_specs=pl.BlockSpec((tm, d), row),
        out_shape=jax.ShapeDtypeStruct((t, d), F32),
        compiler_params=_cparams("parallel"),
        name="merge_out_proj",
    )(x, h, oa, ob, wg, wa, wb, wo)


def _mlp_ple_kernel(x_ref, p_ref, gm_ref, wu_ref, wd_ref, gp_ref, wg_ref, wp_ref, gn_ref,
                    o_ref, *maybe_h_ref, nchunk):
    x = x_ref[...]
    h2 = _rms(x, gm_ref[...]).astype(BF16)
    tf = wu_ref.shape[1] // nchunk
    acc = x
    for c in range(nchunk):
        up = jnp.dot(h2, wu_ref[:, c * tf:(c + 1) * tf], preferred_element_type=F32)
        act = jnp.square(jnp.maximum(up, 0.0)).astype(BF16)
        acc = acc + jnp.dot(act, wd_ref[c * tf:(c + 1) * tf, :], preferred_element_type=F32)
    hn = _rms(acc, gp_ref[...]).astype(BF16)
    gate = _sigmoid(jnp.dot(hn, wg_ref[...], preferred_element_type=F32))
    e = jnp.dot(p_ref[...].astype(BF16), wp_ref[...], preferred_element_type=F32)
    xo = acc + gate * e
    o_ref[...] = xo
    if maybe_h_ref:
        maybe_h_ref[0][...] = _rms(xo, gn_ref[...]).astype(BF16)


def _mlp_ple(x, p, g_mlp, wu, wd, g_ple, wg, wp, g_next, tm=512, nchunk=4):
    t, d = x.shape
    pd = p.shape[1]
    row = lambda i: (i, 0)
    full = lambda i: (0, 0)
    resident = lambda shape: pl.BlockSpec(shape, full, pipeline_mode=pl.Buffered(1))
    emit_next = g_next is not None
    out_shape = [jax.ShapeDtypeStruct((t, d), F32)]
    out_specs = [pl.BlockSpec((tm, d), row)]
    if emit_next:
        out_shape.append(jax.ShapeDtypeStruct((t, d), BF16))
        out_specs.append(pl.BlockSpec((tm, d), row))
    gn = (g_next if emit_next else g_ple).reshape(1, d)
    res = pl.pallas_call(
        functools.partial(_mlp_ple_kernel, nchunk=nchunk),
        grid=(t // tm,),
        in_specs=[pl.BlockSpec((tm, d), row), pl.BlockSpec((tm, pd), row),
                  resident((1, d)), resident(wu.shape), resident(wd.shape),
                  resident((1, d)), resident(wg.shape), resident(wp.shape), resident((1, d))],
        out_specs=out_specs,
        out_shape=out_shape,
        compiler_params=_cparams("parallel"),
        name="mlp_ple",
    )(x, p, g_mlp.reshape(1, d), wu, wd, g_ple.reshape(1, d), wg, wp, gn)
    return (res[0], res[1]) if emit_next else (res[0], None)


def kernel(x, p, norm_mix, w_in, gla_gate_w2, gla_gate_b, gla_out_norm, moba_q_norm,
           moba_k_norm, w_branch_a, w_branch_b, w_out, norm_mlp, w_up, w_down,
           norm_ple, w_ple_gate, w_ple):
    batch, seq, d = x.shape
    depth = w_in.shape[0]
    t = batch * seq
    x = x.reshape(t, d)

    n_gqk = 2 * GLA_HEADS * GLA_DK
    n_gv = GLA_HEADS * GLA_DV
    n_m = MOBA_HEADS * MOBA_DH
    o_gv = n_gqk
    o_lr = o_gv + n_gv
    o_gr = o_lr + GLA_GATE_RANK
    o_mqk = o_gr + n_gv
    o_mv = o_mqk + 2 * n_m
    o_gate = o_mv + n_m

    slopes = 2.0 ** (-8.0 * jnp.arange(1, MOBA_HEADS + 1, dtype=F32) / MOBA_HEADS)
    slopes = jnp.broadcast_to(slopes[:, None, None],
                              (MOBA_HEADS, 1, MOBA_QBLOCKS_PER_STEP * MOBA_BLOCK))

    h = _norm_cast(x, norm_mix[0])
    for li in range(depth):
        w = w_in[li]
        w_gqk = w[:, :o_gv].astype(BF16)
        w_gv = w[:, o_gv:o_lr].astype(BF16)
        w_lr = jnp.pad(w[:, o_lr:o_gr], ((0, 0), (0, LANES - GLA_GATE_RANK))).astype(BF16)
        w2 = jnp.pad(gla_gate_w2[li], ((0, LANES - GLA_GATE_RANK), (0, 0))).astype(BF16)
        w_gr = w[:, o_gr:o_mqk].astype(BF16)
        w_mq_t = w[:, o_mqk:o_mqk + n_m].T.astype(BF16)
        w_mk = w[:, o_mqk + n_m:o_mv].astype(BF16)
        w_mv_t = w[:, o_mv:o_gate].T.astype(BF16)
        w_gate = w[:, o_gate:].astype(BF16)

        mq_t, mk, mv_t = _moba_proj(h, w_mq_t, w_mk, w_mv_t, moba_q_norm[li], moba_k_norm[li])

        oa = _gla(h, w_gqk, w_gv, w_gr, w_lr, w2, gla_gate_b[li].reshape(1, -1),
                  gla_out_norm[li], batch, seq)
        ob = _moba(mq_t, mk, mv_t, slopes, batch, seq)

        x = _merge(x, h, oa, ob, w_gate, w_branch_a[li].astype(BF16),
                   w_branch_b[li].astype(BF16), w_out[li].astype(BF16))
        g_next = norm_mix[li + 1] if li + 1 < depth else None
        x, h = _mlp_ple(x, p[li].reshape(t, -1), norm_mlp[li], w_up[li].astype(BF16),
                        w_down[li].astype(BF16), norm_ple[li], w_ple_gate[li].astype(BF16),
                        w_ple[li].astype(BF16), g_next)
    return x.reshape(batch, seq, d)
```

```python
import functools

import jax
import jax.numpy as jnp
from jax import lax
from jax.experimental import pallas as pl
from jax.experimental.pallas import tpu as pltpu

F32 = jnp.float32
BF16 = jnp.bfloat16

EPS = 1e-6
GLA_HEADS = 4
GLA_DK = 128
GLA_DV = 256
GLA_GATE_RANK = 16
GLA_GATE_TAU = 16.0
GLA_CHUNK = 64
GLA_ROW_BLOCK = 256
MOBA_HEADS = 8
MOBA_DH = 128
MOBA_BLOCK = 256
MOBA_TOPK = 3
MOBA_HEADS_PER_STEP = 4
MOBA_QBLOCKS_PER_STEP = 1
MOBA_ONES_ROWS = 16
MOBA_DV_AUG = MOBA_DH + MOBA_ONES_ROWS
LOG2E = 1.4426950408889634

LANES = 128
VMEM_LIMIT = 48 * 1024 * 1024
NEG_BIG = -1e30

NT_DIMS = (((1,), (1,)), ((), ()))
TN_DIMS = (((0,), (0,)), ((), ()))


def _cparams(*sem):
    return pltpu.CompilerParams(dimension_semantics=sem, vmem_limit_bytes=VMEM_LIMIT)


def _rms(x, g):
    return x * lax.rsqrt(jnp.mean(x * x, axis=-1, keepdims=True) + EPS) * g


def _sigmoid(x):
    return 1.0 / (1.0 + jnp.exp(-x))


def _norm_kernel(x_ref, g_ref, o_ref):
    o_ref[...] = _rms(x_ref[...], g_ref[...]).astype(o_ref.dtype)


def _norm_cast(x, g, tm=1024):
    t, d = x.shape
    return pl.pallas_call(
        _norm_kernel,
        grid=(t // tm,),
        in_specs=[pl.BlockSpec((tm, d), lambda i: (i, 0)),
                  pl.BlockSpec((1, d), lambda i: (0, 0))],
        out_specs=pl.BlockSpec((tm, d), lambda i: (i, 0)),
        out_shape=jax.ShapeDtypeStruct((t, d), BF16),
        compiler_params=_cparams("parallel"),
        name="norm_cast",
    )(x, g.reshape(1, d))


def _split_w_in_kernel(w_ref, gqk_ref, gv_ref, lr_ref, gr_ref, mqt_ref, mk_ref, mvt_ref, gate_ref):
    tr = w_ref.shape[0]
    offs = [0]
    for ref in (gqk_ref, gv_ref):
        offs.append(offs[-1] + ref.shape[1])
    offs.append(offs[-1] + GLA_GATE_RANK)
    for n in (gr_ref.shape[1], mqt_ref.shape[0], mk_ref.shape[1], mvt_ref.shape[0],
              gate_ref.shape[1]):
        offs.append(offs[-1] + n)
    piece = lambda i: w_ref[:, offs[i]:offs[i + 1]]
    gqk_ref[...] = piece(0).astype(BF16)
    gv_ref[...] = piece(1).astype(BF16)
    lr_ref[...] = jnp.concatenate(
        [piece(2), jnp.zeros((tr, LANES - GLA_GATE_RANK), F32)], axis=1).astype(BF16)
    gr_ref[...] = piece(3).astype(BF16)
    mqt_ref[...] = piece(4).T.astype(BF16)
    mk_ref[...] = piece(5).astype(BF16)
    mvt_ref[...] = piece(6).T.astype(BF16)
    gate_ref[...] = piece(7).astype(BF16)


def _split_w_in(w_in, tr=256):
    depth, d, n_in = w_in.shape
    n_gqk = 2 * GLA_HEADS * GLA_DK
    n_gv = GLA_HEADS * GLA_DV
    n_m = MOBA_HEADS * MOBA_DH
    assert n_in == n_gqk + n_gv + GLA_GATE_RANK + n_gv + 3 * n_m + 2 * d
    rows = lambda l, i: (l, i, 0)
    cols = lambda l, i: (l, 0, i)
    row_major = lambda n: ((depth, d, n), pl.BlockSpec((None, tr, n), rows))
    transposed = lambda n: ((depth, n, d), pl.BlockSpec((None, n, tr), cols))
    outs = [row_major(n_gqk), row_major(n_gv), row_major(LANES), row_major(n_gv),
            transposed(n_m), row_major(n_m), transposed(n_m), row_major(2 * d)]
    return pl.pallas_call(
        _split_w_in_kernel,
        grid=(depth, d // tr),
        in_specs=[pl.BlockSpec((None, tr, n_in), rows)],
        out_specs=[spec for _, spec in outs],
        out_shape=[jax.ShapeDtypeStruct(shape, BF16) for shape, _ in outs],
        compiler_params=_cparams("parallel", "parallel"),
        name="split_w_in",
    )(w_in)


def _moba_proj_kernel(h_ref, wqt_ref, wk_ref, wvt_ref, gq_ref, gk_ref, qt_ref, k_ref, vt_ref,
                      *, scale):
    dh = MOBA_DH
    blk = qt_ref.shape[2]
    h = h_ref[...]
    nheads = k_ref.shape[1] // dh

    q_t = lax.dot_general(wqt_ref[...], h, NT_DIMS, preferred_element_type=F32)
    gq = gq_ref[...] * scale
    segs = []
    for hh in range(nheads):
        seg = q_t[hh * dh:(hh + 1) * dh, :]
        ms = jnp.mean(seg * seg, axis=0, keepdims=True)
        segs.append((seg * lax.rsqrt(ms + EPS) * gq).astype(qt_ref.dtype))
    q_t = jnp.concatenate(segs, axis=0)

    k = jnp.dot(h, wk_ref[...], preferred_element_type=F32)
    gk = gk_ref[...]
    for hh in range(nheads):
        k_ref[:, hh * dh:(hh + 1) * dh] = _rms(k[:, hh * dh:(hh + 1) * dh], gk).astype(k_ref.dtype)

    v_t = lax.dot_general(wvt_ref[...], h, NT_DIMS, preferred_element_type=F32)
    ones = jnp.ones((MOBA_ONES_ROWS, v_t.shape[1]), vt_ref.dtype)
    parts = []
    for hh in range(nheads):
        parts += [v_t[hh * dh:(hh + 1) * dh, :].astype(vt_ref.dtype), ones]
    v_t = jnp.concatenate(parts, axis=0)
    for c in range(qt_ref.shape[0]):
        qt_ref[c] = q_t[:, c * blk:(c + 1) * blk]
        vt_ref[c] = v_t[:, c * blk:(c + 1) * blk]


def _moba_proj(h, wq_t, wk, wv_t, gq, gk, tm=512, blk=MOBA_BLOCK):
    t, d = h.shape
    n = wk.shape[1]
    full = lambda i: (0, 0)
    resident = lambda shape: pl.BlockSpec(shape, full, pipeline_mode=pl.Buffered(1))
    nv = n // MOBA_DH * MOBA_DV_AUG
    q_spec = pl.BlockSpec((tm // blk, n, blk), lambda i: (i, 0, 0))
    v_spec = pl.BlockSpec((tm // blk, nv, blk), lambda i: (i, 0, 0))
    q_shape = jax.ShapeDtypeStruct((t // blk, n, blk), BF16)
    v_shape = jax.ShapeDtypeStruct((t // blk, nv, blk), BF16)
    return pl.pallas_call(
        functools.partial(_moba_proj_kernel, scale=MOBA_DH ** -0.5 * LOG2E),
        grid=(t // tm,),
        in_specs=[pl.BlockSpec((tm, d), lambda i: (i, 0)),
                  resident((n, d)), resident((d, n)), resident((n, d)),
                  resident((MOBA_DH, 1)), resident((1, MOBA_DH))],
        out_specs=[q_spec, pl.BlockSpec((tm, n), lambda i: (i, 0)), v_spec],
        out_shape=[q_shape, jax.ShapeDtypeStruct((t, n), BF16), v_shape],
        compiler_params=_cparams("parallel"),
        name="moba_qkv_proj",
    )(h, wq_t, wk, wv_t, gq.reshape(MOBA_DH, 1), gk.reshape(1, MOBA_DH))


def _gla_kernel(h_ref, wqk_ref, wv_ref, wr_ref, wlr_ref, w2_ref, b_ref, gn_ref, o_ref,
                st_ref, tri_ref, *, rb):
    n_k = GLA_HEADS * GLA_DK
    h = h_ref[...]
    lr = jnp.dot(h, wlr_ref[...], preferred_element_type=F32)
    qk_all = jnp.dot(h, wqk_ref[...], preferred_element_type=F32)
    z = jnp.dot(lr.astype(BF16), w2_ref[...], preferred_element_type=F32) + b_ref[...]
    la_all = (jnp.minimum(z, 0.0) - jnp.log(1.0 + jnp.exp(-jnp.abs(z)))) * (1.0 / GLA_GATE_TAU)

    c = GLA_CHUNK
    nc = rb // c
    shift = c.bit_length() - 1
    row = lax.broadcasted_iota(jnp.int32, (rb, rb), 0)
    col = lax.broadcasted_iota(jnp.int32, (rb, rb), 1)
    same_chunk = (row >> shift) == (col >> shift)
    causal = same_chunk & (row >= col)

    @pl.when(pl.program_id(1) == 0)
    def _():
        st_ref[...] = jnp.zeros_like(st_ref)
        tri_ref[...] = causal.astype(BF16)

    tri = tri_ref[...]
    heads = range(GLA_HEADS)
    kcs = [slice(hh * GLA_DK, (hh + 1) * GLA_DK) for hh in heads]
    vcs = [slice(hh * GLA_DV, (hh + 1) * GLA_DV) for hh in heads]
    la_hi = la_all.astype(BF16)
    la_lo = (la_all - la_hi.astype(F32)).astype(BF16)
    v_all = jnp.dot(h, wv_ref[...], preferred_element_type=F32).astype(BF16)
    b = (jnp.dot(tri, la_hi, preferred_element_type=F32)
         + jnp.dot(tri, la_lo, preferred_element_type=F32))
    g_r = jnp.dot(h, wr_ref[...], preferred_element_type=F32)
    gr_all = g_r * _sigmoid(g_r)
    b_last = jnp.concatenate(
        [jnp.broadcast_to(b[ci * c + c - 1:ci * c + c, :], (c, b.shape[1])) for ci in range(nc)],
        axis=0)
    q_all = qk_all[:, :n_k]
    k_all = qk_all[:, n_k:]
    decay = jnp.exp(b)
    qd = (q_all * decay * (GLA_DK ** -0.5)).astype(BF16)
    kd = (k_all * jnp.exp(-b)).astype(BF16)
    kl = (k_all * jnp.exp(b_last - b)).astype(BF16)
    a = [lax.dot_general(qd[:, kcs[hh]], kd[:, kcs[hh]], NT_DIMS, preferred_element_type=F32)
         for hh in heads]
    chunks = [slice(ci * c, (ci + 1) * c) for ci in range(nc)]
    kv_t = [[lax.dot_general(v_all[rs, vcs[hh]], kl[rs, kcs[hh]], TN_DIMS,
                             preferred_element_type=F32) for hh in heads] for rs in chunks]
    o_intra = [jnp.dot(jnp.where(causal, a[hh], 0.0).astype(BF16), v_all[:, vcs[hh]],
                       preferred_element_type=F32) for hh in heads]
    st = [st_ref[hh] for hh in heads]
    o_inter = []
    for ci, rs in enumerate(chunks):
        o_inter.append([lax.dot_general(qd[rs, kcs[hh]], st[hh].astype(BF16), NT_DIMS,
                                        preferred_element_type=F32) for hh in heads])
        last = ci * c + c - 1
        st = [st[hh] * decay[last:last + 1, kcs[hh]] + kv_t[ci][hh] for hh in heads]
    for hh in heads:
        st_ref[hh] = st[hh]
    for ci, rs in enumerate(chunks):
        for hh in heads:
            o = o_intra[hh][rs] + o_inter[ci][hh]
            o_ref[rs, vcs[hh]] = (_rms(o, gn_ref[hh]) * gr_all[rs, vcs[hh]]).astype(o_ref.dtype)


def _gla(h, w_qk, w_v, w_r, w_lr, w2, b, gnorm, batch, seq, rb=GLA_ROW_BLOCK):
    t, d = h.shape
    nblk = seq // rb
    nh = GLA_HEADS
    full = lambda bi, s: (0, 0)
    return pl.pallas_call(
        functools.partial(_gla_kernel, rb=rb),
        grid=(batch, nblk),
        in_specs=[pl.BlockSpec((rb, d), lambda bi, s: (bi * nblk + s, 0)),
                  pl.BlockSpec(w_qk.shape, full), pl.BlockSpec(w_v.shape, full),
                  pl.BlockSpec(w_r.shape, full), pl.BlockSpec(w_lr.shape, full),
                  pl.BlockSpec(w2.shape, full), pl.BlockSpec(b.shape, full),
                  pl.BlockSpec((nh, 1, GLA_DV), lambda bi, s: (0, 0, 0))],
        out_specs=pl.BlockSpec((rb, nh * GLA_DV), lambda bi, s: (bi * nblk + s, 0)),
        out_shape=jax.ShapeDtypeStruct((t, nh * GLA_DV), BF16),
        scratch_shapes=[pltpu.VMEM((nh, GLA_DV, GLA_DK), F32),
                        pltpu.VMEM((rb, rb), BF16)],
        compiler_params=_cparams("parallel", "arbitrary"),
        name="gla_branch",
    )(h, w_qk, w_v, w_r, w_lr, w2, b, gnorm.reshape(nh, 1, GLA_DV))


def _alibi_split(slope):
    c = slope * LOG2E
    c_hi = c.astype(BF16).astype(F32)
    return c_hi, c - c_hi


def _moba_build_keys(k, slope, kaug_ref, kmh_ref, kml_ref, nb):
    blk, dh = MOBA_BLOCK, MOBA_DH
    col = lax.broadcasted_iota(jnp.int32, (blk, dh), 1)
    r_key = lax.broadcasted_iota(jnp.int32, (blk, dh), 0).astype(F32)
    c_hi, c_lo = _alibi_split(slope[:, :dh])
    base = jnp.where((col == nb) | (col == nb + 4), c_hi, 0.0)
    base = jnp.where((col == nb + 1) | (col == nb + 5), c_lo, base)
    base = jnp.where((col == nb + 2) | (col == nb + 3), r_key, base)
    is_offset = (col == nb + 6) | (col == nb + 7)
    ones = jnp.ones((8, blk), BF16)
    sums = []
    for n in range(nb):
        kn = k[n * blk:(n + 1) * blk, :]
        kaug_ref[n * blk:(n + 1) * blk, :dh] = kn
        e = jnp.where(col == n, 1.0, jnp.where(is_offset, float(n * blk), base))
        kaug_ref[n * blk:(n + 1) * blk, dh:] = e.astype(BF16)
        sums.append(jnp.dot(ones, kn, preferred_element_type=F32)[:1])
    km = jnp.concatenate(sums, axis=0) * (1.0 / blk)
    km_hi = km.astype(BF16)
    kmh_ref[...] = km_hi
    kml_ref[...] = (km - km_hi.astype(F32)).astype(BF16)


def _moba_aug_queries(q_t, g_t, slope, first_blk, nb):
    blk, dh = MOBA_BLOCK, MOBA_DH
    qw = q_t.shape[1]
    shift = blk.bit_length() - 1
    nidx = lax.broadcasted_iota(jnp.int32, (nb, qw), 0)
    q_blk = first_blk + (lax.broadcasted_iota(jnp.int32, (nb, qw), 1) >> shift)
    valid = nidx < q_blk
    g = jnp.where(valid, g_t, -jnp.inf)
    rank = jnp.zeros((nb, qw), jnp.int32)
    for m in range(nb):
        gm = g[m:m + 1, :]
        beats = (gm > g) | ((gm == g) & (nidx > m))
        rank = rank + beats.astype(jnp.int32)
    selb = jnp.where(valid & (rank < MOBA_TOPK), 0.0, NEG_BIG)

    ridx = lax.broadcasted_iota(jnp.int32, (dh - nb, qw), 0) + nb
    lane = lax.broadcasted_iota(jnp.int32, (dh - nb, qw), 1)
    r_t = (lane & (blk - 1)).astype(F32)
    off = ((first_blk + (lane >> shift)) * blk).astype(F32)
    c_hi, c_lo = _alibi_split(slope)
    rest = jnp.where((ridx == nb) | (ridx == nb + 1), -r_t, 0.0)
    rest = jnp.where((ridx == nb + 2) | (ridx == nb + 6), c_hi, rest)
    rest = jnp.where((ridx == nb + 3) | (ridx == nb + 7), c_lo, rest)
    rest = jnp.where((ridx == nb + 4) | (ridx == nb + 5), -off, rest)
    x_t = jnp.concatenate([selb, rest], axis=0)
    return jnp.concatenate([q_t, x_t.astype(BF16)], axis=0)


def _moba_kernel(qt_ref, k_ref, vt_ref, slope_ref, o_ref, kaug_ref, qaug_ref, kmh_ref, kml_ref,
                 s0_ref, s1_ref, acc_ref, *, nb, hp, qb):
    blk, dh, dva = MOBA_BLOCK, MOBA_DH, MOBA_DV_AUG
    qw = qb * blk
    step = pl.program_id(2)
    first_blk = step * qb
    hs = [slice(hh * dh, (hh + 1) * dh) for hh in range(hp)]
    vs = [slice(hh * dva, (hh + 1) * dva) for hh in range(hp)]

    @pl.when(step == 0)
    def _():
        for hh in range(hp):
            _moba_build_keys(k_ref[:, hs[hh]], slope_ref[hh][:, :blk], kaug_ref.at[hh],
                             kmh_ref.at[hh], kml_ref.at[hh], nb)

    q_ts = [jnp.concatenate([qt_ref[c, hs[hh], :] for c in range(qb)], axis=1)
            for hh in range(hp)]
    lanes = [slice(c * blk, (c + 1) * blk) for c in range(qb)]
    blk_rows = [pl.ds(pl.multiple_of((first_blk + c) * blk, blk), blk) for c in range(qb)]
    owns = [[jnp.dot(k_ref[blk_rows[c], hs[hh]], q_ts[hh][:, lanes[c]],
                     preferred_element_type=F32) for c in range(qb)]
            for hh in range(hp)]
    gates = [jnp.dot(kmh_ref[hh], q_ts[hh], preferred_element_type=F32)
             + jnp.dot(kml_ref[hh], q_ts[hh], preferred_element_type=F32)
             for hh in range(hp)]
    for hh in range(hp):
        qaug_ref[hh] = _moba_aug_queries(q_ts[hh], gates[hh], slope_ref[hh], first_blk, nb)

    def past_scores(hh, j, qcols=slice(None)):
        rows = pl.ds(pl.multiple_of(j * blk, blk), blk)
        s = jnp.dot(kaug_ref[hh, rows, :], qaug_ref[hh, :, qcols], preferred_element_type=F32)
        return s, jnp.max(s, axis=0, keepdims=True)

    def update(hh, s, s_max, j, m, qcols=slice(None)):
        m_new = jnp.maximum(m, s_max)
        alpha = jnp.exp2(m - m_new)
        p = jnp.exp2((s - m_new).astype(BF16))
        pv = jnp.dot(vt_ref[j, vs[hh], :], p, preferred_element_type=F32)
        acc_ref[hh, :, qcols] = alpha * acc_ref[hh, :, qcols] + pv
        return m_new

    kk = lax.broadcasted_iota(jnp.int32, (blk, blk), 0)
    qq = lax.broadcasted_iota(jnp.int32, (blk, blk), 1)
    dist = (qq - kk).astype(F32)
    ms = []
    for hh in range(hp):
        acc_ref[hh] = jnp.zeros((dva, qw), F32)
        c_log2 = slope_ref[hh][:, :blk] * LOG2E
        m_blocks = []
        for c in range(qb):
            own = jnp.where(qq >= kk, owns[hh][c] - c_log2 * dist, NEG_BIG)
            m_blocks.append(update(hh, own, jnp.max(own, axis=0, keepdims=True), first_blk + c,
                                   jnp.full((1, blk), -jnp.inf, F32), lanes[c]))
        ms.append(m_blocks)
    max0 = []
    for hh in range(hp):
        s, s_max = past_scores(hh, 0)
        s0_ref[hh] = s
        max0.append(s_max)
    carry = []
    for hh in range(hp):
        for c in range(qb - 1):
            later = slice((c + 1) * blk, qw)
            s, s_max = past_scores(hh, first_blk + c, later)
            m_later = update(hh, s, s_max, first_blk + c, jnp.concatenate(ms[hh][c + 1:], axis=1),
                             later)
            ms[hh][c + 1:] = [m_later[:, n * blk:(n + 1) * blk] for n in range(qb - 1 - c)]
        carry.append((jnp.concatenate(ms[hh], axis=1), max0[hh]))

    def body(jj, carry):
        j0 = 2 * jj
        j2 = jnp.minimum(j0 + 2, nb - 1)
        ms = [c[0] for c in carry]
        max0 = [c[1] for c in carry]
        max1 = []
        for hh in range(hp):
            s, s_max = past_scores(hh, j0 + 1)
            s1_ref[hh] = s
            max1.append(s_max)
        for hh in range(hp):
            ms[hh] = update(hh, s0_ref[hh], max0[hh], j0, ms[hh])
        for hh in range(hp):
            s, max0[hh] = past_scores(hh, j2)
            s0_ref[hh] = s
        for hh in range(hp):
            ms[hh] = update(hh, s1_ref[hh], max1[hh], j0 + 1, ms[hh])
        return tuple(zip(ms, max0))

    lax.fori_loop(0, (first_blk + 1) // 2, body, tuple(carry))
    for hh in range(hp):
        acc = acc_ref[hh]
        o_ref[:, hs[hh]] = (acc[:dh] / acc[dh:dh + 1]).T.astype(o_ref.dtype)


def _moba(q_t, k, v_t, slopes, batch, seq, hp=MOBA_HEADS_PER_STEP, qb=MOBA_QBLOCKS_PER_STEP):
    t = k.shape[0]
    h, dh, dva, blk = MOBA_HEADS, MOBA_DH, MOBA_DV_AUG, MOBA_BLOCK
    nb = seq // blk
    ns = nb // qb
    qw = qb * blk
    hg = h // hp
    return pl.pallas_call(
        functools.partial(_moba_kernel, nb=nb, hp=hp, qb=qb),
        grid=(batch, hg, ns),
        in_specs=[pl.BlockSpec((qb, hp * dh, blk), lambda b, g, i: (b * ns + i, g, 0)),
                  pl.BlockSpec((seq, hp * dh), lambda b, g, i: (b, g)),
                  pl.BlockSpec((nb, hp * dva, blk), lambda b, g, i: (b, g, 0)),
                  pl.BlockSpec((hp, 1, qw), lambda b, g, i: (g, 0, 0))],
        out_specs=pl.BlockSpec((qw, hp * dh), lambda b, g, i: (b * ns + i, g)),
        out_shape=jax.ShapeDtypeStruct((t, h * dh), BF16),
        scratch_shapes=[pltpu.VMEM((hp, seq, 2 * dh), BF16),
                        pltpu.VMEM((hp, 2 * dh, qw), BF16),
                        pltpu.VMEM((hp, nb, dh), BF16),
                        pltpu.VMEM((hp, nb, dh), BF16),
                        pltpu.VMEM((hp, blk, qw), F32),
                        pltpu.VMEM((hp, blk, qw), F32),
                        pltpu.VMEM((hp, dva, qw), F32)],
        compiler_params=_cparams("parallel", "parallel", "arbitrary"),
        name="moba_attn",
    )(q_t, k, v_t, slopes)


def _merge_kernel(x_ref, h_ref, oa_ref, ob_ref, wg_ref, wa_ref, wb_ref, wo_ref, o_ref):
    d = x_ref.shape[1]
    gates = _sigmoid(jnp.dot(h_ref[...], wg_ref[...], preferred_element_type=F32))
    ya = jnp.dot(oa_ref[...], wa_ref[...], preferred_element_type=F32)
    yb = jnp.dot(ob_ref[...], wb_ref[...], preferred_element_type=F32)
    y = gates[:, :d] * ya + gates[:, d:] * yb
    o_ref[...] = x_ref[...] + jnp.dot(y.astype(BF16), wo_ref[...], preferred_element_type=F32)


def _merge(x, h, oa, ob, wg, wa, wb, wo, tm=512):
    t, d = x.shape
    row = lambda i: (i, 0)
    full = lambda i: (0, 0)
    return pl.pallas_call(
        _merge_kernel,
        grid=(t // tm,),
        in_specs=[pl.BlockSpec((tm, d), row), pl.BlockSpec((tm, d), row),
                  pl.BlockSpec((tm, d), row), pl.BlockSpec((tm, d), row),
                  pl.BlockSpec((d, 2 * d), full), pl.BlockSpec((d, d), full),
                  pl.BlockSpec((d, d), full), pl.BlockSpec((d, d), full)],
        out_specs=pl.BlockSpec((tm, d), row),
        out_shape=jax.ShapeDtypeStruct((t, d), F32),
        compiler_params=_cparams("parallel"),
        name="merge_out_proj",
    )(x, h, oa, ob, wg, wa, wb, wo)


def _mlp_ple_kernel(x_ref, p_ref, gm_ref, wu_ref, wd_ref, gp_ref, wg_ref, wp_ref, gn_ref,
                    o_ref, *maybe_h_ref, nchunk):
    x = x_ref[...]
    h2 = _rms(x, gm_ref[...]).astype(BF16)
    tf = wu_ref.shape[1] // nchunk
    acc = x
    for c in range(nchunk):
        up = jnp.dot(h2, wu_ref[:, c * tf:(c + 1) * tf], preferred_element_type=F32)
        act = jnp.square(jnp.maximum(up, 0.0)).astype(BF16)
        acc = acc + jnp.dot(act, wd_ref[c * tf:(c + 1) * tf, :], preferred_element_type=F32)
    hn = _rms(acc, gp_ref[...]).astype(BF16)
    gate = _sigmoid(jnp.dot(hn, wg_ref[...], preferred_element_type=F32))
    e = jnp.dot(p_ref[...].astype(BF16), wp_ref[...], preferred_element_type=F32)
    xo = acc + gate * e
    o_ref[...] = xo
    if maybe_h_ref:
        maybe_h_ref[0][...] = _rms(xo, gn_ref[...]).astype(BF16)


def _mlp_ple(x, p, g_mlp, wu, wd, g_ple, wg, wp, g_next, tm=512, nchunk=4):
    t, d = x.shape
    pd = p.shape[1]
    row = lambda i: (i, 0)
    full = lambda i: (0, 0)
    resident = lambda shape: pl.BlockSpec(shape, full, pipeline_mode=pl.Buffered(1))
    emit_next = g_next is not None
    out_shape = [jax.ShapeDtypeStruct((t, d), F32)]
    out_specs = [pl.BlockSpec((tm, d), row)]
    if emit_next:
        out_shape.append(jax.ShapeDtypeStruct((t, d), BF16))
        out_specs.append(pl.BlockSpec((tm, d), row))
    gn = (g_next if emit_next else g_ple).reshape(1, d)
    res = pl.pallas_call(
        functools.partial(_mlp_ple_kernel, nchunk=nchunk),
        grid=(t // tm,),
        in_specs=[pl.BlockSpec((tm, d), row), pl.BlockSpec((tm, pd), row),
                  resident((1, d)), resident(wu.shape), resident(wd.shape),
                  resident((1, d)), resident(wg.shape), resident(wp.shape), resident((1, d))],
        out_specs=out_specs,
        out_shape=out_shape,
        compiler_params=_cparams("parallel"),
        name="mlp_ple",
    )(x, p, g_mlp.reshape(1, d), wu, wd, g_ple.reshape(1, d), wg, wp, gn)
    return (res[0], res[1]) if emit_next else (res[0], None)


def kernel(x, p, norm_mix, w_in, gla_gate_w2, gla_gate_b, gla_out_norm, moba_q_norm,
           moba_k_norm, w_branch_a, w_branch_b, w_out, norm_mlp, w_up, w_down,
           norm_ple, w_ple_gate, w_ple):
    batch, seq, d = x.shape
    depth = w_in.shape[0]
    t = batch * seq
    x = x.reshape(t, d)

    slopes = 2.0 ** (-8.0 * jnp.arange(1, MOBA_HEADS + 1, dtype=F32) / MOBA_HEADS)
    slopes = jnp.broadcast_to(slopes[:, None, None],
                              (MOBA_HEADS, 1, MOBA_QBLOCKS_PER_STEP * MOBA_BLOCK))

    w_gqk, w_gv, w_lr, w_gr, w_mq_t, w_mk, w_mv_t, w_gate = _split_w_in(w_in)
    w2 = jnp.pad(gla_gate_w2, ((0, 0), (0, LANES - GLA_GATE_RANK), (0, 0))).astype(BF16)

    h = _norm_cast(x, norm_mix[0])
    for li in range(depth):
        mq_t, mk, mv_t = _moba_proj(h, w_mq_t[li], w_mk[li], w_mv_t[li], moba_q_norm[li],
                                    moba_k_norm[li])

        oa = _gla(h, w_gqk[li], w_gv[li], w_gr[li], w_lr[li], w2[li],
                  gla_gate_b[li].reshape(1, -1), gla_out_norm[li], batch, seq)
        ob = _moba(mq_t, mk, mv_t, slopes, batch, seq)

        x = _merge(x, h, oa, ob, w_gate[li], w_branch_a[li].astype(BF16),
                   w_branch_b[li].astype(BF16), w_out[li].astype(BF16))
        g_next = norm_mix[li + 1] if li + 1 < depth else None
        x, h = _mlp_ple(x, p[li].reshape(t, -1), norm_mlp[li], w_up[li].astype(BF16),
                        w_down[li].astype(BF16), norm_ple[li], w_ple_gate[li].astype(BF16),
                        w_ple[li].astype(BF16), g_next)
    return x.reshape(batch, seq, d)
```

```python
import functools

import jax
import jax.numpy as jnp
from jax import lax
from jax.experimental import pallas as pl
from jax.experimental.pallas import tpu as pltpu

F32 = jnp.float32
BF16 = jnp.bfloat16

EPS = 1e-6
GLA_HEADS = 4
GLA_DK = 128
GLA_DV = 256
GLA_GATE_RANK = 16
GLA_GATE_TAU = 16.0
GLA_CHUNK = 64
GLA_ROW_BLOCK = 256
MOBA_HEADS = 8
MOBA_DH = 128
MOBA_BLOCK = 256
MOBA_TOPK = 3
MOBA_HEADS_PER_STEP = 4
MOBA_QBLOCKS_PER_STEP = 1
MOBA_ONES_ROWS = 16
MOBA_DV_AUG = MOBA_DH + MOBA_ONES_ROWS
LOG2E = 1.4426950408889634

LANES = 128
VMEM_LIMIT = 48 * 1024 * 1024
NEG_BIG = -1e30

NT_DIMS = (((1,), (1,)), ((), ()))
TN_DIMS = (((0,), (0,)), ((), ()))


def _cparams(*sem):
    return pltpu.CompilerParams(dimension_semantics=sem, vmem_limit_bytes=VMEM_LIMIT)


def _layer_weight(stacked, li):
    return pl.BlockSpec((None,) + stacked.shape[1:], lambda *_: (li, 0, 0),
                        pipeline_mode=pl.Buffered(1))


def _rms(x, g):
    return x * lax.rsqrt(jnp.mean(x * x, axis=-1, keepdims=True) + EPS) * g


def _sigmoid(x):
    return 1.0 / (1.0 + jnp.exp(-x))


def _norm_kernel(x_ref, g_ref, o_ref):
    o_ref[...] = _rms(x_ref[...], g_ref[...]).astype(o_ref.dtype)


def _norm_cast(x, g, tm=1024):
    t, d = x.shape
    return pl.pallas_call(
        _norm_kernel,
        grid=(t // tm,),
        in_specs=[pl.BlockSpec((tm, d), lambda i: (i, 0)),
                  pl.BlockSpec((1, d), lambda i: (0, 0))],
        out_specs=pl.BlockSpec((tm, d), lambda i: (i, 0)),
        out_shape=jax.ShapeDtypeStruct((t, d), BF16),
        compiler_params=_cparams("parallel"),
        name="norm_cast",
    )(x, g.reshape(1, d))


def _split_w_in_kernel(wt_ref, gqk_ref, gv_ref, lr_ref, gr_ref, mqt_ref, mk_ref, mvt_ref, gate_ref):
    offs = [0]
    for ref in (gqk_ref, gv_ref):
        offs.append(offs[-1] + ref.shape[1])
    offs.append(offs[-1] + GLA_GATE_RANK)
    for n in (gr_ref.shape[1], mqt_ref.shape[0], mk_ref.shape[1], mvt_ref.shape[0],
              gate_ref.shape[1]):
        offs.append(offs[-1] + n)
    piece_t = lambda i: wt_ref[offs[i]:offs[i + 1], :]
    gqk_ref[...] = piece_t(0).T.astype(BF16)
    gv_ref[...] = piece_t(1).T.astype(BF16)
    lr_t = wt_ref[offs[2]:offs[2] + LANES, :].T
    lane = lax.broadcasted_iota(jnp.int32, lr_t.shape, 1)
    lr_ref[...] = jnp.where(lane < GLA_GATE_RANK, lr_t, 0.0).astype(BF16)
    gr_ref[...] = piece_t(3).T.astype(BF16)
    mqt_ref[...] = piece_t(4).astype(BF16)
    mk_ref[...] = piece_t(5).T.astype(BF16)
    mvt_ref[...] = piece_t(6).astype(BF16)
    gate_ref[...] = piece_t(7).T.astype(BF16)


def _split_w_in(w_in, tc=256):
    depth, d, n_in = w_in.shape
    n_gqk = 2 * GLA_HEADS * GLA_DK
    n_gv = GLA_HEADS * GLA_DV
    n_m = MOBA_HEADS * MOBA_DH
    assert n_in == n_gqk + n_gv + GLA_GATE_RANK + n_gv + 3 * n_m + 2 * d
    rows = lambda l, i: (l, i, 0)
    cols = lambda l, i: (l, 0, i)
    row_major = lambda n: ((depth, d, n), pl.BlockSpec((None, tc, n), rows))
    transposed = lambda n: ((depth, n, d), pl.BlockSpec((None, n, tc), cols))
    outs = [row_major(n_gqk), row_major(n_gv), row_major(LANES), row_major(n_gv),
            transposed(n_m), row_major(n_m), transposed(n_m), row_major(2 * d)]
    return pl.pallas_call(
        _split_w_in_kernel,
        grid=(depth, d // tc),
        in_specs=[pl.BlockSpec((None, n_in, tc), cols)],
        out_specs=[spec for _, spec in outs],
        out_shape=[jax.ShapeDtypeStruct(shape, BF16) for shape, _ in outs],
        compiler_params=_cparams("parallel", "parallel"),
        name="split_w_in",
    )(jnp.swapaxes(w_in, 1, 2))


def _moba_proj_kernel(h_ref, wqt_ref, wk_ref, wvt_ref, gq_ref, gk_ref, qt_ref, k_ref, vt_ref,
                      *, scale):
    dh = MOBA_DH
    blk = qt_ref.shape[2]
    h = h_ref[...]
    nheads = k_ref.shape[1] // dh

    q_t = lax.dot_general(wqt_ref[...], h, NT_DIMS, preferred_element_type=F32)
    gq = gq_ref[...] * scale
    segs = []
    for hh in range(nheads):
        seg = q_t[hh * dh:(hh + 1) * dh, :]
        ms = jnp.mean(seg * seg, axis=0, keepdims=True)
        segs.append((seg * lax.rsqrt(ms + EPS) * gq).astype(qt_ref.dtype))
    q_t = jnp.concatenate(segs, axis=0)

    k = jnp.dot(h, wk_ref[...], preferred_element_type=F32)
    gk = gk_ref[...]
    for hh in range(nheads):
        k_ref[:, hh * dh:(hh + 1) * dh] = _rms(k[:, hh * dh:(hh + 1) * dh], gk).astype(k_ref.dtype)

    v_t = lax.dot_general(wvt_ref[...], h, NT_DIMS, preferred_element_type=F32)
    ones = jnp.ones((MOBA_ONES_ROWS, v_t.shape[1]), vt_ref.dtype)
    parts = []
    for hh in range(nheads):
        parts += [v_t[hh * dh:(hh + 1) * dh, :].astype(vt_ref.dtype), ones]
    v_t = jnp.concatenate(parts, axis=0)
    for c in range(qt_ref.shape[0]):
        qt_ref[c] = q_t[:, c * blk:(c + 1) * blk]
        vt_ref[c] = v_t[:, c * blk:(c + 1) * blk]


def _moba_proj(h, wq_t, wk, wv_t, gq, gk, li, tm=512, blk=MOBA_BLOCK):
    t, d = h.shape
    n = wk.shape[2]
    full = lambda i: (0, 0)
    resident = lambda shape: pl.BlockSpec(shape, full, pipeline_mode=pl.Buffered(1))
    nv = n // MOBA_DH * MOBA_DV_AUG
    q_spec = pl.BlockSpec((tm // blk, n, blk), lambda i: (i, 0, 0))
    v_spec = pl.BlockSpec((tm // blk, nv, blk), lambda i: (i, 0, 0))
    q_shape = jax.ShapeDtypeStruct((t // blk, n, blk), BF16)
    v_shape = jax.ShapeDtypeStruct((t // blk, nv, blk), BF16)
    return pl.pallas_call(
        functools.partial(_moba_proj_kernel, scale=MOBA_DH ** -0.5 * LOG2E),
        grid=(t // tm,),
        in_specs=[pl.BlockSpec((tm, d), lambda i: (i, 0)),
                  _layer_weight(wq_t, li), _layer_weight(wk, li), _layer_weight(wv_t, li),
                  resident((MOBA_DH, 1)), resident((1, MOBA_DH))],
        out_specs=[q_spec, pl.BlockSpec((tm, n), lambda i: (i, 0)), v_spec],
        out_shape=[q_shape, jax.ShapeDtypeStruct((t, n), BF16), v_shape],
        compiler_params=_cparams("parallel"),
        name="moba_qkv_proj",
    )(h, wq_t, wk, wv_t, gq.reshape(MOBA_DH, 1), gk.reshape(1, MOBA_DH))


def _gla_kernel(h_ref, wqk_ref, wv_ref, wr_ref, wlr_ref, w2_ref, b_ref, gn_ref, o_ref,
                st_ref, tri_ref, *, rb):
    n_k = GLA_HEADS * GLA_DK
    h = h_ref[...]
    lr = jnp.dot(h, wlr_ref[...], preferred_element_type=F32)
    qk_all = jnp.dot(h, wqk_ref[...], preferred_element_type=F32)
    z = jnp.dot(lr.astype(BF16), w2_ref[...], preferred_element_type=F32) + b_ref[...]
    la_all = (jnp.minimum(z, 0.0) - jnp.log(1.0 + jnp.exp(-jnp.abs(z)))) * (1.0 / GLA_GATE_TAU)

    c = GLA_CHUNK
    nc = rb // c
    shift = c.bit_length() - 1
    row = lax.broadcasted_iota(jnp.int32, (rb, rb), 0)
    col = lax.broadcasted_iota(jnp.int32, (rb, rb), 1)
    same_chunk = (row >> shift) == (col >> shift)
    causal = same_chunk & (row >= col)

    @pl.when(pl.program_id(1) == 0)
    def _():
        st_ref[...] = jnp.zeros_like(st_ref)
        tri_ref[...] = causal.astype(BF16)

    tri = tri_ref[...]
    heads = range(GLA_HEADS)
    kcs = [slice(hh * GLA_DK, (hh + 1) * GLA_DK) for hh in heads]
    vcs = [slice(hh * GLA_DV, (hh + 1) * GLA_DV) for hh in heads]
    la_hi = la_all.astype(BF16)
    la_lo = (la_all - la_hi.astype(F32)).astype(BF16)
    v_all = jnp.dot(h, wv_ref[...], preferred_element_type=F32).astype(BF16)
    b = (jnp.dot(tri, la_hi, preferred_element_type=F32)
         + jnp.dot(tri, la_lo, preferred_element_type=F32))
    g_r = jnp.dot(h, wr_ref[...], preferred_element_type=F32)
    gr_all = g_r * _sigmoid(g_r)
    b_last = jnp.concatenate(
        [jnp.broadcast_to(b[ci * c + c - 1:ci * c + c, :], (c, b.shape[1])) for ci in range(nc)],
        axis=0)
    q_all = qk_all[:, :n_k]
    k_all = qk_all[:, n_k:]
    decay = jnp.exp(b)
    qd = (q_all * decay * (GLA_DK ** -0.5)).astype(BF16)
    kd = (k_all * jnp.exp(-b)).astype(BF16)
    kl = (k_all * jnp.exp(b_last - b)).astype(BF16)
    a = [lax.dot_general(qd[:, kcs[hh]], kd[:, kcs[hh]], NT_DIMS, preferred_element_type=F32)
         for hh in heads]
    chunks = [slice(ci * c, (ci + 1) * c) for ci in range(nc)]
    kv_t = [[lax.dot_general(v_all[rs, vcs[hh]], kl[rs, kcs[hh]], TN_DIMS,
                             preferred_element_type=F32) for hh in heads] for rs in chunks]
    o_intra = [jnp.dot(jnp.where(causal, a[hh], 0.0).astype(BF16), v_all[:, vcs[hh]],
                       preferred_element_type=F32) for hh in heads]
    st = [st_ref[hh] for hh in heads]
    o_inter = []
    for ci, rs in enumerate(chunks):
        o_inter.append([lax.dot_general(qd[rs, kcs[hh]], st[hh].astype(BF16), NT_DIMS,
                                        preferred_element_type=F32) for hh in heads])
        last = ci * c + c - 1
        st = [st[hh] * decay[last:last + 1, kcs[hh]] + kv_t[ci][hh] for hh in heads]
    for hh in heads:
        st_ref[hh] = st[hh]
    for ci, rs in enumerate(chunks):
        for hh in heads:
            o = o_intra[hh][rs] + o_inter[ci][hh]
            o_ref[rs, vcs[hh]] = (_rms(o, gn_ref[hh]) * gr_all[rs, vcs[hh]]).astype(o_ref.dtype)


def _gla(h, w_qk, w_v, w_r, w_lr, w2, b, gnorm, li, batch, seq, rb=GLA_ROW_BLOCK):
    t, d = h.shape
    nblk = seq // rb
    nh = GLA_HEADS
    full = lambda bi, s: (0, 0)
    return pl.pallas_call(
        functools.partial(_gla_kernel, rb=rb),
        grid=(batch, nblk),
        in_specs=[pl.BlockSpec((rb, d), lambda bi, s: (bi * nblk + s, 0)),
                  _layer_weight(w_qk, li), _layer_weight(w_v, li), _layer_weight(w_r, li),
                  _layer_weight(w_lr, li), _layer_weight(w2, li), pl.BlockSpec(b.shape, full),
                  pl.BlockSpec((nh, 1, GLA_DV), lambda bi, s: (0, 0, 0))],
        out_specs=pl.BlockSpec((rb, nh * GLA_DV), lambda bi, s: (bi * nblk + s, 0)),
        out_shape=jax.ShapeDtypeStruct((t, nh * GLA_DV), BF16),
        scratch_shapes=[pltpu.VMEM((nh, GLA_DV, GLA_DK), F32),
                        pltpu.VMEM((rb, rb), BF16)],
        compiler_params=_cparams("parallel", "arbitrary"),
        name="gla_branch",
    )(h, w_qk, w_v, w_r, w_lr, w2, b, gnorm.reshape(nh, 1, GLA_DV))


def _alibi_split(slope):
    c = slope * LOG2E
    c_hi = c.astype(BF16).astype(F32)
    return c_hi, c - c_hi


def _moba_build_keys(k, slope, kaug_ref, kmh_ref, kml_ref, nb):
    blk, dh = MOBA_BLOCK, MOBA_DH
    col = lax.broadcasted_iota(jnp.int32, (blk, dh), 1)
    r_key = lax.broadcasted_iota(jnp.int32, (blk, dh), 0).astype(F32)
    c_hi, c_lo = _alibi_split(slope[:, :dh])
    base = jnp.where((col == nb) | (col == nb + 4), c_hi, 0.0)
    base = jnp.where((col == nb + 1) | (col == nb + 5), c_lo, base)
    base = jnp.where((col == nb + 2) | (col == nb + 3), r_key, base)
    is_offset = (col == nb + 6) | (col == nb + 7)
    ones = jnp.ones((8, blk), BF16)
    sums = []
    for n in range(nb):
        kn = k[n * blk:(n + 1) * blk, :]
        kaug_ref[n * blk:(n + 1) * blk, :dh] = kn
        e = jnp.where(col == n, 1.0, jnp.where(is_offset, float(n * blk), base))
        kaug_ref[n * blk:(n + 1) * blk, dh:] = e.astype(BF16)
        sums.append(jnp.dot(ones, kn, preferred_element_type=F32)[:1])
    km = jnp.concatenate(sums, axis=0) * (1.0 / blk)
    km_hi = km.astype(BF16)
    kmh_ref[...] = km_hi
    kml_ref[...] = (km - km_hi.astype(F32)).astype(BF16)


def _moba_aug_queries(q_t, g_t, slope, first_blk, nb):
    blk, dh = MOBA_BLOCK, MOBA_DH
    qw = q_t.shape[1]
    shift = blk.bit_length() - 1
    nidx = lax.broadcasted_iota(jnp.int32, (nb, qw), 0)
    q_blk = first_blk + (lax.broadcasted_iota(jnp.int32, (nb, qw), 1) >> shift)
    valid = nidx < q_blk
    g = jnp.where(valid, g_t, -jnp.inf)
    rank = jnp.zeros((nb, qw), jnp.int32)
    for m in range(nb):
        gm = g[m:m + 1, :]
        beats = (gm > g) | ((gm == g) & (nidx > m))
        rank = rank + beats.astype(jnp.int32)
    selb = jnp.where(valid & (rank < MOBA_TOPK), 0.0, NEG_BIG)

    ridx = lax.broadcasted_iota(jnp.int32, (dh - nb, qw), 0) + nb
    lane = lax.broadcasted_iota(jnp.int32, (dh - nb, qw), 1)
    r_t = (lane & (blk - 1)).astype(F32)
    off = ((first_blk + (lane >> shift)) * blk).astype(F32)
    c_hi, c_lo = _alibi_split(slope)
    rest = jnp.where((ridx == nb) | (ridx == nb + 1), -r_t, 0.0)
    rest = jnp.where((ridx == nb + 2) | (ridx == nb + 6), c_hi, rest)
    rest = jnp.where((ridx == nb + 3) | (ridx == nb + 7), c_lo, rest)
    rest = jnp.where((ridx == nb + 4) | (ridx == nb + 5), -off, rest)
    x_t = jnp.concatenate([selb, rest], axis=0)
    return jnp.concatenate([q_t, x_t.astype(BF16)], axis=0)


def _moba_kernel(qt_ref, k_ref, vt_ref, slope_ref, o_ref, kaug_ref, qaug_ref, kmh_ref, kml_ref,
                 s0_ref, s1_ref, acc_ref, *, nb, hp, qb):
    blk, dh, dva = MOBA_BLOCK, MOBA_DH, MOBA_DV_AUG
    qw = qb * blk
    step = pl.program_id(2)
    first_blk = step * qb
    hs = [slice(hh * dh, (hh + 1) * dh) for hh in range(hp)]
    vs = [slice(hh * dva, (hh + 1) * dva) for hh in range(hp)]

    @pl.when(step == 0)
    def _():
        for hh in range(hp):
            _moba_build_keys(k_ref[:, hs[hh]], slope_ref[hh][:, :blk], kaug_ref.at[hh],
                             kmh_ref.at[hh], kml_ref.at[hh], nb)

    q_ts = [jnp.concatenate([qt_ref[c, hs[hh], :] for c in range(qb)], axis=1)
            for hh in range(hp)]
    lanes = [slice(c * blk, (c + 1) * blk) for c in range(qb)]
    blk_rows = [pl.ds(pl.multiple_of((first_blk + c) * blk, blk), blk) for c in range(qb)]
    owns = [[jnp.dot(k_ref[blk_rows[c], hs[hh]], q_ts[hh][:, lanes[c]],
                     preferred_element_type=F32) for c in range(qb)]
            for hh in range(hp)]
    gates = [jnp.dot(kmh_ref[hh], q_ts[hh], preferred_element_type=F32)
             + jnp.dot(kml_ref[hh], q_ts[hh], preferred_element_type=F32)
             for hh in range(hp)]
    for hh in range(hp):
        qaug_ref[hh] = _moba_aug_queries(q_ts[hh], gates[hh], slope_ref[hh], first_blk, nb)

    def past_scores(hh, j, qcols=slice(None)):
        rows = pl.ds(pl.multiple_of(j * blk, blk), blk)
        s = jnp.dot(kaug_ref[hh, rows, :], qaug_ref[hh, :, qcols], preferred_element_type=F32)
        return s, jnp.max(s, axis=0, keepdims=True)

    def update(hh, s, s_max, j, m, qcols=slice(None)):
        m_new = jnp.maximum(m, s_max)
        alpha = jnp.exp2(m - m_new)
        p = jnp.exp2((s - m_new).astype(BF16))
        pv = jnp.dot(vt_ref[j, vs[hh], :], p, preferred_element_type=F32)
        acc_ref[hh, :, qcols] = alpha * acc_ref[hh, :, qcols] + pv
        return m_new

    kk = lax.broadcasted_iota(jnp.int32, (blk, blk), 0)
    qq = lax.broadcasted_iota(jnp.int32, (blk, blk), 1)
    dist = (qq - kk).astype(F32)
    ms = []
    for hh in range(hp):
        acc_ref[hh] = jnp.zeros((dva, qw), F32)
        c_log2 = slope_ref[hh][:, :blk] * LOG2E
        m_blocks = []
        for c in range(qb):
            own = jnp.where(qq >= kk, owns[hh][c] - c_log2 * dist, NEG_BIG)
            m_blocks.append(update(hh, own, jnp.max(own, axis=0, keepdims=True), first_blk + c,
                                   jnp.full((1, blk), -jnp.inf, F32), lanes[c]))
        ms.append(m_blocks)
    max0 = []
    for hh in range(hp):
        s, s_max = past_scores(hh, 0)
        s0_ref[hh] = s
        max0.append(s_max)
    carry = []
    for hh in range(hp):
        for c in range(qb - 1):
            later = slice((c + 1) * blk, qw)
            s, s_max = past_scores(hh, first_blk + c, later)
            m_later = update(hh, s, s_max, first_blk + c, jnp.concatenate(ms[hh][c + 1:], axis=1),
                             later)
            ms[hh][c + 1:] = [m_later[:, n * blk:(n + 1) * blk] for n in range(qb - 1 - c)]
        carry.append((jnp.concatenate(ms[hh], axis=1), max0[hh]))

    def body(jj, carry):
        j0 = 2 * jj
        j2 = jnp.minimum(j0 + 2, nb - 1)
        ms = [c[0] for c in carry]
        max0 = [c[1] for c in carry]
        max1 = []
        for hh in range(hp):
            s, s_max = past_scores(hh, j0 + 1)
            s1_ref[hh] = s
            max1.append(s_max)
        for hh in range(hp):
            ms[hh] = update(hh, s0_ref[hh], max0[hh], j0, ms[hh])
        for hh in range(hp):
            s, max0[hh] = past_scores(hh, j2)
            s0_ref[hh] = s
        for hh in range(hp):
            ms[hh] = update(hh, s1_ref[hh], max1[hh], j0 + 1, ms[hh])
        return tuple(zip(ms, max0))

    lax.fori_loop(0, (first_blk + 1) // 2, body, tuple(carry))
    for hh in range(hp):
        acc = acc_ref[hh]
        o_ref[:, hs[hh]] = (acc[:dh] / acc[dh:dh + 1]).T.astype(o_ref.dtype)


def _moba(q_t, k, v_t, slopes, batch, seq, hp=MOBA_HEADS_PER_STEP, qb=MOBA_QBLOCKS_PER_STEP):
    t = k.shape[0]
    h, dh, dva, blk = MOBA_HEADS, MOBA_DH, MOBA_DV_AUG, MOBA_BLOCK
    nb = seq // blk
    ns = nb // qb
    qw = qb * blk
    hg = h // hp
    return pl.pallas_call(
        functools.partial(_moba_kernel, nb=nb, hp=hp, qb=qb),
        grid=(batch, hg, ns),
        in_specs=[pl.BlockSpec((qb, hp * dh, blk), lambda b, g, i: (b * ns + i, g, 0)),
                  pl.BlockSpec((seq, hp * dh), lambda b, g, i: (b, g)),
                  pl.BlockSpec((nb, hp * dva, blk), lambda b, g, i: (b, g, 0)),
                  pl.BlockSpec((hp, 1, qw), lambda b, g, i: (g, 0, 0))],
        out_specs=pl.BlockSpec((qw, hp * dh), lambda b, g, i: (b * ns + i, g)),
        out_shape=jax.ShapeDtypeStruct((t, h * dh), BF16),
        scratch_shapes=[pltpu.VMEM((hp, seq, 2 * dh), BF16),
                        pltpu.VMEM((hp, 2 * dh, qw), BF16),
                        pltpu.VMEM((hp, nb, dh), BF16),
                        pltpu.VMEM((hp, nb, dh), BF16),
                        pltpu.VMEM((hp, blk, qw), F32),
                        pltpu.VMEM((hp, blk, qw), F32),
                        pltpu.VMEM((hp, dva, qw), F32)],
        compiler_params=_cparams("parallel", "parallel", "arbitrary"),
        name="moba_attn",
    )(q_t, k, v_t, slopes)


def _merge_kernel(x_ref, h_ref, oa_ref, ob_ref, wg_ref, wa_ref, wb_ref, wo_ref, o_ref):
    d = x_ref.shape[1]
    gates = _sigmoid(jnp.dot(h_ref[...], wg_ref[...], preferred_element_type=F32))
    ya = jnp.dot(oa_ref[...], wa_ref[...], preferred_element_type=F32)
    yb = jnp.dot(ob_ref[...], wb_ref[...], preferred_element_type=F32)
    y = gates[:, :d] * ya + gates[:, d:] * yb
    o_ref[...] = x_ref[...] + jnp.dot(y.astype(BF16), wo_ref[...], preferred_element_type=F32)


def _merge(x, h, oa, ob, wg, wa, wb, wo, li, tm=512):
    t, d = x.shape
    row = lambda i: (i, 0)
    return pl.pallas_call(
        _merge_kernel,
        grid=(t // tm,),
        in_specs=[pl.BlockSpec((tm, d), row), pl.BlockSpec((tm, d), row),
                  pl.BlockSpec((tm, d), row), pl.BlockSpec((tm, d), row),
                  _layer_weight(wg, li), _layer_weight(wa, li),
                  _layer_weight(wb, li), _layer_weight(wo, li)],
        out_specs=pl.BlockSpec((tm, d), row),
        out_shape=jax.ShapeDtypeStruct((t, d), F32),
        compiler_params=_cparams("parallel"),
        name="merge_out_proj",
    )(x, h, oa, ob, wg, wa, wb, wo)


def _mlp_ple_kernel(x_ref, p_ref, gm_ref, wu_ref, wd_ref, gp_ref, wg_ref, wp_ref, gn_ref,
                    o_ref, *maybe_h_ref, nchunk):
    x = x_ref[...]
    h2 = _rms(x, gm_ref[...]).astype(BF16)
    tf = wu_ref.shape[1] // nchunk
    acc = x
    for c in range(nchunk):
        up = jnp.dot(h2, wu_ref[:, c * tf:(c + 1) * tf], preferred_element_type=F32)
        act = jnp.square(jnp.maximum(up, 0.0)).astype(BF16)
        acc = acc + jnp.dot(act, wd_ref[c * tf:(c + 1) * tf, :], preferred_element_type=F32)
    hn = _rms(acc, gp_ref[...]).astype(BF16)
    gate = _sigmoid(jnp.dot(hn, wg_ref[...], preferred_element_type=F32))
    e = jnp.dot(p_ref[...].astype(BF16), wp_ref[...], preferred_element_type=F32)
    xo = acc + gate * e
    o_ref[...] = xo
    if maybe_h_ref:
        maybe_h_ref[0][...] = _rms(xo, gn_ref[...]).astype(BF16)


def _mlp_ple(x, p, g_mlp, wu, wd, g_ple, wg, wp, g_next, li, tm=512, nchunk=4):
    t, d = x.shape
    pd = p.shape[2]
    row = lambda i: (i, 0)
    full = lambda i: (0, 0)
    resident = lambda shape: pl.BlockSpec(shape, full, pipeline_mode=pl.Buffered(1))
    emit_next = g_next is not None
    out_shape = [jax.ShapeDtypeStruct((t, d), F32)]
    out_specs = [pl.BlockSpec((tm, d), row)]
    if emit_next:
        out_shape.append(jax.ShapeDtypeStruct((t, d), BF16))
        out_specs.append(pl.BlockSpec((tm, d), row))
    gn = (g_next if emit_next else g_ple).reshape(1, d)
    res = pl.pallas_call(
        functools.partial(_mlp_ple_kernel, nchunk=nchunk),
        grid=(t // tm,),
        in_specs=[pl.BlockSpec((tm, d), row), pl.BlockSpec((None, tm, pd), lambda i: (li, i, 0)),
                  resident((1, d)), _layer_weight(wu, li), _layer_weight(wd, li),
                  resident((1, d)), _layer_weight(wg, li), _layer_weight(wp, li),
                  resident((1, d))],
        out_specs=out_specs,
        out_shape=out_shape,
        compiler_params=_cparams("parallel"),
        name="mlp_ple",
    )(x, p, g_mlp.reshape(1, d), wu, wd, g_ple.reshape(1, d), wg, wp, gn)
    return (res[0], res[1]) if emit_next else (res[0], None)


def kernel(x, p, norm_mix, w_in, gla_gate_w2, gla_gate_b, gla_out_norm, moba_q_norm,
           moba_k_norm, w_branch_a, w_branch_b, w_out, norm_mlp, w_up, w_down,
           norm_ple, w_ple_gate, w_ple):
    batch, seq, d = x.shape
    depth = w_in.shape[0]
    t = batch * seq
    x = x.reshape(t, d)

    slopes = 2.0 ** (-8.0 * jnp.arange(1, MOBA_HEADS + 1, dtype=F32) / MOBA_HEADS)
    slopes = jnp.broadcast_to(slopes[:, None, None],
                              (MOBA_HEADS, 1, MOBA_QBLOCKS_PER_STEP * MOBA_BLOCK))

    w_gqk, w_gv, w_lr, w_gr, w_mq_t, w_mk, w_mv_t, w_gate = _split_w_in(w_in)
    w2 = jnp.pad(gla_gate_w2, ((0, 0), (0, LANES - GLA_GATE_RANK), (0, 0))).astype(BF16)
    w_a, w_b, w_o = (w.astype(BF16) for w in (w_branch_a, w_branch_b, w_out))
    w_u, w_d, w_pg, w_pe = (w.astype(BF16) for w in (w_up, w_down, w_ple_gate, w_ple))
    p = p.reshape(depth, t, -1)

    h = _norm_cast(x, norm_mix[0])
    for li in range(depth):
        mq_t, mk, mv_t = _moba_proj(h, w_mq_t, w_mk, w_mv_t, moba_q_norm[li], moba_k_norm[li], li)

        oa = _gla(h, w_gqk, w_gv, w_gr, w_lr, w2, gla_gate_b[li].reshape(1, -1),
                  gla_out_norm[li], li, batch, seq)
        ob = _moba(mq_t, mk, mv_t, slopes, batch, seq)

        x = _merge(x, h, oa, ob, w_gate, w_a, w_b, w_o, li)
        g_next = norm_mix[li + 1] if li + 1 < depth else None
        x, h = _mlp_ple(x, p, norm_mlp[li], w_u, w_d, norm_ple[li], w_pg, w_pe, g_next, li)
    return x.reshape(batch, seq, d)
```

```python
import functools

import jax
import jax.numpy as jnp
from jax import lax
from jax.experimental import pallas as pl
from jax.experimental.pallas import tpu as pltpu

F32 = jnp.float32
BF16 = jnp.bfloat16

EPS = 1e-6
GLA_HEADS = 4
GLA_DK = 128
GLA_DV = 256
GLA_GATE_RANK = 16
GLA_GATE_TAU = 16.0
GLA_CHUNK = 64
GLA_ROW_BLOCK = 256
MOBA_HEADS = 8
MOBA_DH = 128
MOBA_BLOCK = 256
MOBA_TOPK = 3
MOBA_HEADS_PER_STEP = 4
MOBA_QBLOCKS_PER_STEP = 1
MOBA_ONES_ROWS = 16
MOBA_DV_AUG = MOBA_DH + MOBA_ONES_ROWS
LOG2E = 1.4426950408889634

LANES = 128
VMEM_LIMIT = 48 * 1024 * 1024
NEG_BIG = -1e30

NT_DIMS = (((1,), (1,)), ((), ()))
TN_DIMS = (((0,), (0,)), ((), ()))


def _cparams(*sem):
    return pltpu.CompilerParams(dimension_semantics=sem, vmem_limit_bytes=VMEM_LIMIT)


def _layer_weight(stacked, li):
    return pl.BlockSpec((None,) + stacked.shape[1:], lambda *_: (li, 0, 0),
                        pipeline_mode=pl.Buffered(1))


def _rms(x, g):
    return x * lax.rsqrt(jnp.mean(x * x, axis=-1, keepdims=True) + EPS) * g


def _sigmoid(x):
    return 1.0 / (1.0 + jnp.exp(-x))


def _norm_kernel(x_ref, g_ref, o_ref):
    o_ref[...] = _rms(x_ref[...], g_ref[...]).astype(o_ref.dtype)


def _norm_cast(x, g, tm=1024):
    t, d = x.shape
    return pl.pallas_call(
        _norm_kernel,
        grid=(t // tm,),
        in_specs=[pl.BlockSpec((tm, d), lambda i: (i, 0)),
                  pl.BlockSpec((1, d), lambda i: (0, 0))],
        out_specs=pl.BlockSpec((tm, d), lambda i: (i, 0)),
        out_shape=jax.ShapeDtypeStruct((t, d), BF16),
        compiler_params=_cparams("parallel"),
        name="norm_cast",
    )(x, g.reshape(1, d))


def _split_w_in_kernel(wt_ref, gqk_ref, gv_ref, lr_ref, gr_ref, mqt_ref, mk_ref, mvt_ref, gate_ref):
    offs = [0]
    for ref in (gqk_ref, gv_ref):
        offs.append(offs[-1] + ref.shape[1])
    offs.append(offs[-1] + GLA_GATE_RANK)
    for n in (gr_ref.shape[1], mqt_ref.shape[0], mk_ref.shape[1], mvt_ref.shape[0],
              gate_ref.shape[1]):
        offs.append(offs[-1] + n)
    piece_t = lambda i: wt_ref[offs[i]:offs[i + 1], :]
    gqk_ref[...] = piece_t(0).T.astype(BF16)
    gv_ref[...] = piece_t(1).T.astype(BF16)
    lr_t = wt_ref[offs[2]:offs[2] + LANES, :].T
    lane = lax.broadcasted_iota(jnp.int32, lr_t.shape, 1)
    lr_ref[...] = jnp.where(lane < GLA_GATE_RANK, lr_t, 0.0).astype(BF16)
    gr_ref[...] = piece_t(3).T.astype(BF16)
    mqt_ref[...] = piece_t(4).astype(BF16)
    mk_ref[...] = piece_t(5).T.astype(BF16)
    mvt_ref[...] = piece_t(6).astype(BF16)
    gate_ref[...] = piece_t(7).T.astype(BF16)


def _split_w_in(w_in, tc=256):
    depth, d, n_in = w_in.shape
    n_gqk = 2 * GLA_HEADS * GLA_DK
    n_gv = GLA_HEADS * GLA_DV
    n_m = MOBA_HEADS * MOBA_DH
    assert n_in == n_gqk + n_gv + GLA_GATE_RANK + n_gv + 3 * n_m + 2 * d
    rows = lambda l, i: (l, i, 0)
    cols = lambda l, i: (l, 0, i)
    row_major = lambda n: ((depth, d, n), pl.BlockSpec((None, tc, n), rows))
    transposed = lambda n: ((depth, n, d), pl.BlockSpec((None, n, tc), cols))
    outs = [row_major(n_gqk), row_major(n_gv), row_major(LANES), row_major(n_gv),
            transposed(n_m), row_major(n_m), transposed(n_m), row_major(2 * d)]
    return pl.pallas_call(
        _split_w_in_kernel,
        grid=(depth, d // tc),
        in_specs=[pl.BlockSpec((None, n_in, tc), cols)],
        out_specs=[spec for _, spec in outs],
        out_shape=[jax.ShapeDtypeStruct(shape, BF16) for shape, _ in outs],
        compiler_params=_cparams("parallel", "parallel"),
        name="split_w_in",
    )(jnp.swapaxes(w_in, 1, 2))


def _moba_proj_kernel(h_ref, wqt_ref, wk_ref, wvt_ref, gq_ref, gk_ref, wlr_ref, w2_ref, b_ref,
                      qt_ref, k_ref, vt_ref, lah_ref, lal_ref, *, scale):
    dh = MOBA_DH
    blk = qt_ref.shape[2]
    h = h_ref[...]
    nheads = k_ref.shape[1] // dh

    lr = jnp.dot(h, wlr_ref[...], preferred_element_type=F32)
    q_t = lax.dot_general(wqt_ref[...], h, NT_DIMS, preferred_element_type=F32)
    gq = gq_ref[...] * scale
    segs = []
    for hh in range(nheads):
        seg = q_t[hh * dh:(hh + 1) * dh, :]
        ms = jnp.mean(seg * seg, axis=0, keepdims=True)
        segs.append((seg * lax.rsqrt(ms + EPS) * gq).astype(qt_ref.dtype))
    q_t = jnp.concatenate(segs, axis=0)

    z = jnp.dot(lr.astype(BF16), w2_ref[...], preferred_element_type=F32) + b_ref[...]
    k = jnp.dot(h, wk_ref[...], preferred_element_type=F32)
    gk = gk_ref[...]
    for hh in range(nheads):
        k_ref[:, hh * dh:(hh + 1) * dh] = _rms(k[:, hh * dh:(hh + 1) * dh], gk).astype(k_ref.dtype)

    v_t = lax.dot_general(wvt_ref[...], h, NT_DIMS, preferred_element_type=F32)
    ones = jnp.ones((MOBA_ONES_ROWS, v_t.shape[1]), vt_ref.dtype)
    parts = []
    for hh in range(nheads):
        parts += [v_t[hh * dh:(hh + 1) * dh, :].astype(vt_ref.dtype), ones]
    v_t = jnp.concatenate(parts, axis=0)
    for c in range(qt_ref.shape[0]):
        qt_ref[c] = q_t[:, c * blk:(c + 1) * blk]
        vt_ref[c] = v_t[:, c * blk:(c + 1) * blk]

    log_a = (jnp.minimum(z, 0.0) - jnp.log(1.0 + jnp.exp(-jnp.abs(z)))) * (1.0 / GLA_GATE_TAU)
    la_hi = log_a.astype(BF16)
    lah_ref[...] = la_hi
    lal_ref[...] = (log_a - la_hi.astype(F32)).astype(BF16)


def _moba_proj(h, wq_t, wk, wv_t, gq, gk, w_lr, w2, b, li, tm=512, blk=MOBA_BLOCK):
    t, d = h.shape
    n = wk.shape[2]
    n_la = w2.shape[2]
    full = lambda i: (0, 0)
    resident = lambda shape: pl.BlockSpec(shape, full, pipeline_mode=pl.Buffered(1))
    nv = n // MOBA_DH * MOBA_DV_AUG
    q_spec = pl.BlockSpec((tm // blk, n, blk), lambda i: (i, 0, 0))
    v_spec = pl.BlockSpec((tm // blk, nv, blk), lambda i: (i, 0, 0))
    q_shape = jax.ShapeDtypeStruct((t // blk, n, blk), BF16)
    v_shape = jax.ShapeDtypeStruct((t // blk, nv, blk), BF16)
    return pl.pallas_call(
        functools.partial(_moba_proj_kernel, scale=MOBA_DH ** -0.5 * LOG2E),
        grid=(t // tm,),
        in_specs=[pl.BlockSpec((tm, d), lambda i: (i, 0)),
                  _layer_weight(wq_t, li), _layer_weight(wk, li), _layer_weight(wv_t, li),
                  resident((MOBA_DH, 1)), resident((1, MOBA_DH)),
                  _layer_weight(w_lr, li), _layer_weight(w2, li), resident((1, n_la))],
        out_specs=[q_spec, pl.BlockSpec((tm, n), lambda i: (i, 0)), v_spec,
                   pl.BlockSpec((tm, n_la), lambda i: (i, 0)),
                   pl.BlockSpec((tm, n_la), lambda i: (i, 0))],
        out_shape=[q_shape, jax.ShapeDtypeStruct((t, n), BF16), v_shape,
                   jax.ShapeDtypeStruct((t, n_la), BF16), jax.ShapeDtypeStruct((t, n_la), BF16)],
        compiler_params=_cparams("parallel"),
        name="moba_qkv_proj",
    )(h, wq_t, wk, wv_t, gq.reshape(MOBA_DH, 1), gk.reshape(1, MOBA_DH), w_lr, w2, b)


def _gla_kernel(h_ref, lah_ref, lal_ref, wqk_ref, wv_ref, wr_ref, gn_ref, o_ref,
                st_ref, tri_ref, *, rb):
    n_k = GLA_HEADS * GLA_DK
    h = h_ref[...]
    qk_all = jnp.dot(h, wqk_ref[...], preferred_element_type=F32)

    c = GLA_CHUNK
    nc = rb // c
    shift = c.bit_length() - 1
    row = lax.broadcasted_iota(jnp.int32, (rb, rb), 0)
    col = lax.broadcasted_iota(jnp.int32, (rb, rb), 1)
    same_chunk = (row >> shift) == (col >> shift)
    causal = same_chunk & (row >= col)

    @pl.when(pl.program_id(1) == 0)
    def _():
        st_ref[...] = jnp.zeros_like(st_ref)
        tri_ref[...] = causal.astype(BF16)

    tri = tri_ref[...]
    heads = range(GLA_HEADS)
    kcs = [slice(hh * GLA_DK, (hh + 1) * GLA_DK) for hh in heads]
    vcs = [slice(hh * GLA_DV, (hh + 1) * GLA_DV) for hh in heads]
    b = (jnp.dot(tri, lah_ref[...], preferred_element_type=F32)
         + jnp.dot(tri, lal_ref[...], preferred_element_type=F32))
    v_all = jnp.dot(h, wv_ref[...], preferred_element_type=F32).astype(BF16)
    g_r = jnp.dot(h, wr_ref[...], preferred_element_type=F32)
    gr_all = g_r * _sigmoid(g_r)
    b_last = jnp.concatenate(
        [jnp.broadcast_to(b[ci * c + c - 1:ci * c + c, :], (c, b.shape[1])) for ci in range(nc)],
        axis=0)
    q_all = qk_all[:, :n_k]
    k_all = qk_all[:, n_k:]
    decay = jnp.exp(b)
    qd = (q_all * decay * (GLA_DK ** -0.5)).astype(BF16)
    kd = (k_all * jnp.exp(-b)).astype(BF16)
    kl = (k_all * jnp.exp(b_last - b)).astype(BF16)
    a = [lax.dot_general(qd[:, kcs[hh]], kd[:, kcs[hh]], NT_DIMS, preferred_element_type=F32)
         for hh in heads]
    chunks = [slice(ci * c, (ci + 1) * c) for ci in range(nc)]
    kv_t = [[lax.dot_general(v_all[rs, vcs[hh]], kl[rs, kcs[hh]], TN_DIMS,
                             preferred_element_type=F32) for hh in heads] for rs in chunks]
    o_intra = [jnp.dot(jnp.where(causal, a[hh], 0.0).astype(BF16), v_all[:, vcs[hh]],
                       preferred_element_type=F32) for hh in heads]
    st = [st_ref[hh] for hh in heads]
    o_inter = []
    for ci, rs in enumerate(chunks):
        o_inter.append([lax.dot_general(qd[rs, kcs[hh]], st[hh].astype(BF16), NT_DIMS,
                                        preferred_element_type=F32) for hh in heads])
        last = ci * c + c - 1
        st = [st[hh] * decay[last:last + 1, kcs[hh]] + kv_t[ci][hh] for hh in heads]
    for hh in heads:
        st_ref[hh] = st[hh]
    for ci, rs in enumerate(chunks):
        for hh in heads:
            o = o_intra[hh][rs] + o_inter[ci][hh]
            o_ref[rs, vcs[hh]] = (_rms(o, gn_ref[hh]) * gr_all[rs, vcs[hh]]).astype(o_ref.dtype)


def _gla(h, la_hi, la_lo, w_qk, w_v, w_r, gnorm, li, batch, seq, rb=GLA_ROW_BLOCK):
    t, d = h.shape
    nblk = seq // rb
    nh = GLA_HEADS
    rows = lambda bi, s: (bi * nblk + s, 0)
    return pl.pallas_call(
        functools.partial(_gla_kernel, rb=rb),
        grid=(batch, nblk),
        in_specs=[pl.BlockSpec((rb, d), rows), pl.BlockSpec((rb, la_hi.shape[1]), rows),
                  pl.BlockSpec((rb, la_lo.shape[1]), rows),
                  _layer_weight(w_qk, li), _layer_weight(w_v, li), _layer_weight(w_r, li),
                  pl.BlockSpec((nh, 1, GLA_DV), lambda bi, s: (0, 0, 0))],
        out_specs=pl.BlockSpec((rb, nh * GLA_DV), lambda bi, s: (bi * nblk + s, 0)),
        out_shape=jax.ShapeDtypeStruct((t, nh * GLA_DV), BF16),
        scratch_shapes=[pltpu.VMEM((nh, GLA_DV, GLA_DK), F32),
                        pltpu.VMEM((rb, rb), BF16)],
        compiler_params=_cparams("parallel", "arbitrary"),
        name="gla_branch",
    )(h, la_hi, la_lo, w_qk, w_v, w_r, gnorm.reshape(nh, 1, GLA_DV))


def _alibi_split(slope):
    c = slope * LOG2E
    c_hi = c.astype(BF16).astype(F32)
    return c_hi, c - c_hi


def _moba_build_keys(k, slope, kaug_ref, kmh_ref, kml_ref, nb):
    blk, dh = MOBA_BLOCK, MOBA_DH
    col = lax.broadcasted_iota(jnp.int32, (blk, dh), 1)
    r_key = lax.broadcasted_iota(jnp.int32, (blk, dh), 0).astype(F32)
    c_hi, c_lo = _alibi_split(slope[:, :dh])
    base = jnp.where((col == nb) | (col == nb + 4), c_hi, 0.0)
    base = jnp.where((col == nb + 1) | (col == nb + 5), c_lo, base)
    base = jnp.where((col == nb + 2) | (col == nb + 3), r_key, base)
    is_offset = (col == nb + 6) | (col == nb + 7)
    ones = jnp.ones((8, blk), BF16)
    sums = []
    for n in range(nb):
        kn = k[n * blk:(n + 1) * blk, :]
        kaug_ref[n * blk:(n + 1) * blk, :dh] = kn
        e = jnp.where(col == n, 1.0, jnp.where(is_offset, float(n * blk), base))
        kaug_ref[n * blk:(n + 1) * blk, dh:] = e.astype(BF16)
        sums.append(jnp.dot(ones, kn, preferred_element_type=F32)[:1])
    km = jnp.concatenate(sums, axis=0) * (1.0 / blk)
    km_hi = km.astype(BF16)
    kmh_ref[...] = km_hi
    kml_ref[...] = (km - km_hi.astype(F32)).astype(BF16)


def _moba_aug_queries(q_t, g_t, slope, first_blk, nb):
    blk, dh = MOBA_BLOCK, MOBA_DH
    qw = q_t.shape[1]
    shift = blk.bit_length() - 1
    nidx = lax.broadcasted_iota(jnp.int32, (nb, qw), 0)
    q_blk = first_blk + (lax.broadcasted_iota(jnp.int32, (nb, qw), 1) >> shift)
    valid = nidx < q_blk
    g = jnp.where(valid, g_t, -jnp.inf)
    rank = jnp.zeros((nb, qw), jnp.int32)
    for m in range(nb):
        gm = g[m:m + 1, :]
        beats = (gm > g) | ((gm == g) & (nidx > m))
        rank = rank + beats.astype(jnp.int32)
    selb = jnp.where(valid & (rank < MOBA_TOPK), 0.0, NEG_BIG)

    ridx = lax.broadcasted_iota(jnp.int32, (dh - nb, qw), 0) + nb
    lane = lax.broadcasted_iota(jnp.int32, (dh - nb, qw), 1)
    r_t = (lane & (blk - 1)).astype(F32)
    off = ((first_blk + (lane >> shift)) * blk).astype(F32)
    c_hi, c_lo = _alibi_split(slope)
    rest = jnp.where((ridx == nb) | (ridx == nb + 1), -r_t, 0.0)
    rest = jnp.where((ridx == nb + 2) | (ridx == nb + 6), c_hi, rest)
    rest = jnp.where((ridx == nb + 3) | (ridx == nb + 7), c_lo, rest)
    rest = jnp.where((ridx == nb + 4) | (ridx == nb + 5), -off, rest)
    x_t = jnp.concatenate([selb, rest], axis=0)
    return jnp.concatenate([q_t, x_t.astype(BF16)], axis=0)


def _moba_kernel(qt_ref, k_ref, vt_ref, slope_ref, o_ref, kaug_ref, qaug_ref, kmh_ref, kml_ref,
                 s0_ref, s1_ref, acc_ref, *, nb, hp, qb):
    blk, dh, dva = MOBA_BLOCK, MOBA_DH, MOBA_DV_AUG
    qw = qb * blk
    step = pl.program_id(2)
    first_blk = step * qb
    hs = [slice(hh * dh, (hh + 1) * dh) for hh in range(hp)]
    vs = [slice(hh * dva, (hh + 1) * dva) for hh in range(hp)]

    @pl.when(step == 0)
    def _():
        for hh in range(hp):
            _moba_build_keys(k_ref[:, hs[hh]], slope_ref[hh][:, :blk], kaug_ref.at[hh],
                             kmh_ref.at[hh], kml_ref.at[hh], nb)

    q_ts = [jnp.concatenate([qt_ref[c, hs[hh], :] for c in range(qb)], axis=1)
            for hh in range(hp)]
    lanes = [slice(c * blk, (c + 1) * blk) for c in range(qb)]
    blk_rows = [pl.ds(pl.multiple_of((first_blk + c) * blk, blk), blk) for c in range(qb)]
    owns = [[jnp.dot(k_ref[blk_rows[c], hs[hh]], q_ts[hh][:, lanes[c]],
                     preferred_element_type=F32) for c in range(qb)]
            for hh in range(hp)]
    gates = [jnp.dot(kmh_ref[hh], q_ts[hh], preferred_element_type=F32)
             + jnp.dot(kml_ref[hh], q_ts[hh], preferred_element_type=F32)
             for hh in range(hp)]
    for hh in range(hp):
        qaug_ref[hh] = _moba_aug_queries(q_ts[hh], gates[hh], slope_ref[hh], first_blk, nb)

    def past_scores(hh, j, qcols=slice(None)):
        rows = pl.ds(pl.multiple_of(j * blk, blk), blk)
        s = jnp.dot(kaug_ref[hh, rows, :], qaug_ref[hh, :, qcols], preferred_element_type=F32)
        return s, jnp.max(s, axis=0, keepdims=True)

    def update(hh, s, s_max, j, m, qcols=slice(None)):
        m_new = jnp.maximum(m, s_max)
        alpha = jnp.exp2(m - m_new)
        p = jnp.exp2((s - m_new).astype(BF16))
        pv = jnp.dot(vt_ref[j, vs[hh], :], p, preferred_element_type=F32)
        acc_ref[hh, :, qcols] = alpha * acc_ref[hh, :, qcols] + pv
        return m_new

    kk = lax.broadcasted_iota(jnp.int32, (blk, blk), 0)
    qq = lax.broadcasted_iota(jnp.int32, (blk, blk), 1)
    dist = (qq - kk).astype(F32)
    ms = []
    for hh in range(hp):
        acc_ref[hh] = jnp.zeros((dva, qw), F32)
        c_log2 = slope_ref[hh][:, :blk] * LOG2E
        m_blocks = []
        for c in range(qb):
            own = jnp.where(qq >= kk, owns[hh][c] - c_log2 * dist, NEG_BIG)
            m_blocks.append(update(hh, own, jnp.max(own, axis=0, keepdims=True), first_blk + c,
                                   jnp.full((1, blk), -jnp.inf, F32), lanes[c]))
        ms.append(m_blocks)
    max0 = []
    for hh in range(hp):
        s, s_max = past_scores(hh, 0)
        s0_ref[hh] = s
        max0.append(s_max)
    carry = []
    for hh in range(hp):
        for c in range(qb - 1):
            later = slice((c + 1) * blk, qw)
            s, s_max = past_scores(hh, first_blk + c, later)
            m_later = update(hh, s, s_max, first_blk + c, jnp.concatenate(ms[hh][c + 1:], axis=1),
                             later)
            ms[hh][c + 1:] = [m_later[:, n * blk:(n + 1) * blk] for n in range(qb - 1 - c)]
        carry.append((jnp.concatenate(ms[hh], axis=1), max0[hh]))

    def body(jj, carry):
        j0 = 2 * jj
        j2 = jnp.minimum(j0 + 2, nb - 1)
        ms = [c[0] for c in carry]
        max0 = [c[1] for c in carry]
        max1 = []
        for hh in range(hp):
            s, s_max = past_scores(hh, j0 + 1)
            s1_ref[hh] = s
            max1.append(s_max)
        for hh in range(hp):
            ms[hh] = update(hh, s0_ref[hh], max0[hh], j0, ms[hh])
        for hh in range(hp):
            s, max0[hh] = past_scores(hh, j2)
            s0_ref[hh] = s
        for hh in range(hp):
            ms[hh] = update(hh, s1_ref[hh], max1[hh], j0 + 1, ms[hh])
        return tuple(zip(ms, max0))

    n_pairs = (first_blk + 1) // 2
    carry = lax.fori_loop(0, n_pairs // 2, lambda kk, c: body(2 * kk + 1, body(2 * kk, c)),
                          tuple(carry))
    lax.fori_loop(n_pairs // 2 * 2, n_pairs, body, carry)
    for hh in range(hp):
        acc = acc_ref[hh]
        o_ref[:, hs[hh]] = (acc[:dh] / acc[dh:dh + 1]).T.astype(o_ref.dtype)


def _moba(q_t, k, v_t, slopes, batch, seq, hp=MOBA_HEADS_PER_STEP, qb=MOBA_QBLOCKS_PER_STEP):
    t = k.shape[0]
    h, dh, dva, blk = MOBA_HEADS, MOBA_DH, MOBA_DV_AUG, MOBA_BLOCK
    nb = seq // blk
    ns = nb // qb
    qw = qb * blk
    hg = h // hp
    return pl.pallas_call(
        functools.partial(_moba_kernel, nb=nb, hp=hp, qb=qb),
        grid=(batch, hg, ns),
        in_specs=[pl.BlockSpec((qb, hp * dh, blk), lambda b, g, i: (b * ns + i, g, 0)),
                  pl.BlockSpec((seq, hp * dh), lambda b, g, i: (b, g)),
                  pl.BlockSpec((nb, hp * dva, blk), lambda b, g, i: (b, g, 0)),
                  pl.BlockSpec((hp, 1, qw), lambda b, g, i: (g, 0, 0))],
        out_specs=pl.BlockSpec((qw, hp * dh), lambda b, g, i: (b * ns + i, g)),
        out_shape=jax.ShapeDtypeStruct((t, h * dh), BF16),
        scratch_shapes=[pltpu.VMEM((hp, seq, 2 * dh), BF16),
                        pltpu.VMEM((hp, 2 * dh, qw), BF16),
                        pltpu.VMEM((hp, nb, dh), BF16),
                        pltpu.VMEM((hp, nb, dh), BF16),
                        pltpu.VMEM((hp, blk, qw), F32),
                        pltpu.VMEM((hp, blk, qw), F32),
                        pltpu.VMEM((hp, dva, qw), F32)],
        compiler_params=_cparams("parallel", "parallel", "arbitrary"),
        name="moba_attn",
    )(q_t, k, v_t, slopes)


def _merge_kernel(x_ref, h_ref, oa_ref, ob_ref, wg_ref, wa_ref, wb_ref, wo_ref, o_ref):
    d = x_ref.shape[1]
    gates = _sigmoid(jnp.dot(h_ref[...], wg_ref[...], preferred_element_type=F32))
    ya = jnp.dot(oa_ref[...], wa_ref[...], preferred_element_type=F32)
    yb = jnp.dot(ob_ref[...], wb_ref[...], preferred_element_type=F32)
    y = gates[:, :d] * ya + gates[:, d:] * yb
    o_ref[...] = x_ref[...] + jnp.dot(y.astype(BF16), wo_ref[...], preferred_element_type=F32)


def _merge(x, h, oa, ob, wg, wa, wb, wo, li, tm=512):
    t, d = x.shape
    row = lambda i: (i, 0)
    return pl.pallas_call(
        _merge_kernel,
        grid=(t // tm,),
        in_specs=[pl.BlockSpec((tm, d), row), pl.BlockSpec((tm, d), row),
                  pl.BlockSpec((tm, d), row), pl.BlockSpec((tm, d), row),
                  _layer_weight(wg, li), _layer_weight(wa, li),
                  _layer_weight(wb, li), _layer_weight(wo, li)],
        out_specs=pl.BlockSpec((tm, d), row),
        out_shape=jax.ShapeDtypeStruct((t, d), F32),
        compiler_params=_cparams("parallel"),
        name="merge_out_proj",
    )(x, h, oa, ob, wg, wa, wb, wo)


def _mlp_ple_kernel(x_ref, p_ref, gm_ref, wu_ref, wd_ref, gp_ref, wg_ref, wp_ref, gn_ref,
                    o_ref, *maybe_h_ref, nchunk):
    x = x_ref[...]
    h2 = _rms(x, gm_ref[...]).astype(BF16)
    tf = wu_ref.shape[1] // nchunk
    acc = x
    for c in range(nchunk):
        up = jnp.dot(h2, wu_ref[:, c * tf:(c + 1) * tf], preferred_element_type=F32)
        act = jnp.square(jnp.maximum(up, 0.0)).astype(BF16)
        acc = acc + jnp.dot(act, wd_ref[c * tf:(c + 1) * tf, :], preferred_element_type=F32)
    hn = _rms(acc, gp_ref[...]).astype(BF16)
    gate = _sigmoid(jnp.dot(hn, wg_ref[...], preferred_element_type=F32))
    e = jnp.dot(p_ref[...].astype(BF16), wp_ref[...], preferred_element_type=F32)
    xo = acc + gate * e
    o_ref[...] = xo
    if maybe_h_ref:
        maybe_h_ref[0][...] = _rms(xo, gn_ref[...]).astype(BF16)


def _mlp_ple(x, p, g_mlp, wu, wd, g_ple, wg, wp, g_next, li, tm=512, nchunk=4):
    t, d = x.shape
    pd = p.shape[2]
    row = lambda i: (i, 0)
    full = lambda i: (0, 0)
    resident = lambda shape: pl.BlockSpec(shape, full, pipeline_mode=pl.Buffered(1))
    emit_next = g_next is not None
    out_shape = [jax.ShapeDtypeStruct((t, d), F32)]
    out_specs = [pl.BlockSpec((tm, d), row)]
    if emit_next:
        out_shape.append(jax.ShapeDtypeStruct((t, d), BF16))
        out_specs.append(pl.BlockSpec((tm, d), row))
    gn = (g_next if emit_next else g_ple).reshape(1, d)
    res = pl.pallas_call(
        functools.partial(_mlp_ple_kernel, nchunk=nchunk),
        grid=(t // tm,),
        in_specs=[pl.BlockSpec((tm, d), row), pl.BlockSpec((None, tm, pd), lambda i: (li, i, 0)),
                  resident((1, d)), _layer_weight(wu, li), _layer_weight(wd, li),
                  resident((1, d)), _layer_weight(wg, li), _layer_weight(wp, li),
                  resident((1, d))],
        out_specs=out_specs,
        out_shape=out_shape,
        compiler_params=_cparams("parallel"),
        name="mlp_ple",
    )(x, p, g_mlp.reshape(1, d), wu, wd, g_ple.reshape(1, d), wg, wp, gn)
    return (res[0], res[1]) if emit_next else (res[0], None)


def kernel(x, p, norm_mix, w_in, gla_gate_w2, gla_gate_b, gla_out_norm, moba_q_norm,
           moba_k_norm, w_branch_a, w_branch_b, w_out, norm_mlp, w_up, w_down,
           norm_ple, w_ple_gate, w_ple):
    batch, seq, d = x.shape
    depth = w_in.shape[0]
    t = batch * seq
    x = x.reshape(t, d)

    slopes = 2.0 ** (-8.0 * jnp.arange(1, MOBA_HEADS + 1, dtype=F32) / MOBA_HEADS)
    slopes = jnp.broadcast_to(slopes[:, None, None],
                              (MOBA_HEADS, 1, MOBA_QBLOCKS_PER_STEP * MOBA_BLOCK))

    w_gqk, w_gv, w_lr, w_gr, w_mq_t, w_mk, w_mv_t, w_gate = _split_w_in(w_in)
    w2 = jnp.pad(gla_gate_w2, ((0, 0), (0, LANES - GLA_GATE_RANK), (0, 0))).astype(BF16)
    w_a, w_b, w_o = (w.astype(BF16) for w in (w_branch_a, w_branch_b, w_out))
    w_u, w_d, w_pg, w_pe = (w.astype(BF16) for w in (w_up, w_down, w_ple_gate, w_ple))
    p = p.reshape(depth, t, -1)

    h = _norm_cast(x, norm_mix[0])
    for li in range(depth):
        mq_t, mk, mv_t, la_hi, la_lo = _moba_proj(
            h, w_mq_t, w_mk, w_mv_t, moba_q_norm[li], moba_k_norm[li], w_lr, w2,
            gla_gate_b[li].reshape(1, -1), li)

        oa = _gla(h, la_hi, la_lo, w_gqk, w_gv, w_gr, gla_out_norm[li], li, batch, seq)
        ob = _moba(mq_t, mk, mv_t, slopes, batch, seq)

        x = _merge(x, h, oa, ob, w_gate, w_a, w_b, w_o, li)
        g_next = norm_mix[li + 1] if li + 1 < depth else None
        x, h = _mlp_ple(x, p, norm_mlp[li], w_u, w_d, norm_ple[li], w_pg, w_pe, g_next, li)
    return x.reshape(batch, seq, d)
```

```python
import functools

import jax
import jax.numpy as jnp
from jax import lax
from jax.experimental import pallas as pl
from jax.experimental.pallas import tpu as pltpu

F32 = jnp.float32
BF16 = jnp.bfloat16

EPS = 1e-6
GLA_HEADS = 4
GLA_DK = 128
GLA_DV = 256
GLA_GATE_RANK = 16
GLA_GATE_TAU = 16.0
GLA_CHUNK = 64
GLA_ROW_BLOCK = 256
MOBA_HEADS = 8
MOBA_DH = 128
MOBA_BLOCK = 256
MOBA_TOPK = 3
MOBA_HEADS_PER_STEP = 4
MOBA_QBLOCKS_PER_STEP = 1
MOBA_ONES_ROWS = 16
MOBA_DV_AUG = MOBA_DH + MOBA_ONES_ROWS
LOG2E = 1.4426950408889634

LANES = 128
VMEM_LIMIT = 48 * 1024 * 1024
NEG_BIG = -1e30

NT_DIMS = (((1,), (1,)), ((), ()))
TN_DIMS = (((0,), (0,)), ((), ()))


def _cparams(*sem):
    return pltpu.CompilerParams(dimension_semantics=sem, vmem_limit_bytes=VMEM_LIMIT)


def _layer_weight(stacked, li):
    return pl.BlockSpec((None,) + stacked.shape[1:], lambda *_: (li, 0, 0),
                        pipeline_mode=pl.Buffered(1))


def _rms(x, g):
    return x * lax.rsqrt(jnp.mean(x * x, axis=-1, keepdims=True) + EPS) * g


def _sigmoid(x):
    return 1.0 / (1.0 + jnp.exp(-x))


def _norm_kernel(x_ref, g_ref, o_ref):
    o_ref[...] = _rms(x_ref[...], g_ref[...]).astype(o_ref.dtype)


def _norm_cast(x, g, tm=1024):
    t, d = x.shape
    return pl.pallas_call(
        _norm_kernel,
        grid=(t // tm,),
        in_specs=[pl.BlockSpec((tm, d), lambda i: (i, 0)),
                  pl.BlockSpec((1, d), lambda i: (0, 0))],
        out_specs=pl.BlockSpec((tm, d), lambda i: (i, 0)),
        out_shape=jax.ShapeDtypeStruct((t, d), BF16),
        compiler_params=_cparams("parallel"),
        name="norm_cast",
    )(x, g.reshape(1, d))


def _split_w_in_kernel(wt_ref, gqk_ref, gv_ref, lr_ref, gr_ref, mqt_ref, mk_ref, mvt_ref, gate_ref):
    offs = [0]
    for ref in (gqk_ref, gv_ref):
        offs.append(offs[-1] + ref.shape[1])
    offs.append(offs[-1] + GLA_GATE_RANK)
    for n in (gr_ref.shape[1], mqt_ref.shape[0], mk_ref.shape[1], mvt_ref.shape[0],
              gate_ref.shape[1]):
        offs.append(offs[-1] + n)
    piece_t = lambda i: wt_ref[offs[i]:offs[i + 1], :]
    gqk_ref[...] = piece_t(0).T.astype(BF16)
    gv_ref[...] = piece_t(1).T.astype(BF16)
    lr_t = wt_ref[offs[2]:offs[2] + LANES, :].T
    lane = lax.broadcasted_iota(jnp.int32, lr_t.shape, 1)
    lr_ref[...] = jnp.where(lane < GLA_GATE_RANK, lr_t, 0.0).astype(BF16)
    gr_ref[...] = piece_t(3).T.astype(BF16)
    mqt_ref[...] = piece_t(4).astype(BF16)
    mk_ref[...] = piece_t(5).T.astype(BF16)
    mvt_ref[...] = piece_t(6).astype(BF16)
    gate_ref[...] = piece_t(7).T.astype(BF16)


def _split_w_in(w_in, tc=256):
    depth, d, n_in = w_in.shape
    n_gqk = 2 * GLA_HEADS * GLA_DK
    n_gv = GLA_HEADS * GLA_DV
    n_m = MOBA_HEADS * MOBA_DH
    assert n_in == n_gqk + n_gv + GLA_GATE_RANK + n_gv + 3 * n_m + 2 * d
    rows = lambda l, i: (l, i, 0)
    cols = lambda l, i: (l, 0, i)
    row_major = lambda n: ((depth, d, n), pl.BlockSpec((None, tc, n), rows))
    transposed = lambda n: ((depth, n, d), pl.BlockSpec((None, n, tc), cols))
    outs = [row_major(n_gqk), row_major(n_gv), row_major(LANES), row_major(n_gv),
            transposed(n_m), row_major(n_m), transposed(n_m), row_major(2 * d)]
    return pl.pallas_call(
        _split_w_in_kernel,
        grid=(depth, d // tc),
        in_specs=[pl.BlockSpec((None, n_in, tc), cols)],
        out_specs=[spec for _, spec in outs],
        out_shape=[jax.ShapeDtypeStruct(shape, BF16) for shape, _ in outs],
        compiler_params=_cparams("parallel", "parallel"),
        name="split_w_in",
    )(jnp.swapaxes(w_in, 1, 2))


def _moba_proj_kernel(h_ref, wqt_ref, wk_ref, wvt_ref, gq_ref, gk_ref, wlr_ref, w2_ref, b_ref,
                      qt_ref, k_ref, vt_ref, lah_ref, lal_ref, *, scale):
    dh = MOBA_DH
    blk = qt_ref.shape[2]
    h = h_ref[...]
    nheads = k_ref.shape[1] // dh

    lr = jnp.dot(h, wlr_ref[...], preferred_element_type=F32)
    q_t = lax.dot_general(wqt_ref[...], h, NT_DIMS, preferred_element_type=F32)
    gq = gq_ref[...] * scale
    segs = []
    for hh in range(nheads):
        seg = q_t[hh * dh:(hh + 1) * dh, :]
        ms = jnp.mean(seg * seg, axis=0, keepdims=True)
        segs.append((seg * lax.rsqrt(ms + EPS) * gq).astype(qt_ref.dtype))
    q_t = jnp.concatenate(segs, axis=0)

    z = jnp.dot(lr.astype(BF16), w2_ref[...], preferred_element_type=F32) + b_ref[...]
    k = jnp.dot(h, wk_ref[...], preferred_element_type=F32)
    gk = gk_ref[...]
    for hh in range(nheads):
        k_ref[:, hh * dh:(hh + 1) * dh] = _rms(k[:, hh * dh:(hh + 1) * dh], gk).astype(k_ref.dtype)

    v_t = lax.dot_general(wvt_ref[...], h, NT_DIMS, preferred_element_type=F32)
    ones = jnp.ones((MOBA_ONES_ROWS, v_t.shape[1]), vt_ref.dtype)
    parts = []
    for hh in range(nheads):
        parts += [v_t[hh * dh:(hh + 1) * dh, :].astype(vt_ref.dtype), ones]
    v_t = jnp.concatenate(parts, axis=0)
    for c in range(qt_ref.shape[0]):
        qt_ref[c] = q_t[:, c * blk:(c + 1) * blk]
        vt_ref[c] = v_t[:, c * blk:(c + 1) * blk]

    log_a = (jnp.minimum(z, 0.0) - jnp.log(1.0 + jnp.exp(-jnp.abs(z)))) * (1.0 / GLA_GATE_TAU)
    la_hi = log_a.astype(BF16)
    lah_ref[...] = la_hi
    lal_ref[...] = (log_a - la_hi.astype(F32)).astype(BF16)


def _moba_proj(h, wq_t, wk, wv_t, gq, gk, w_lr, w2, b, li, tm=512, blk=MOBA_BLOCK):
    t, d = h.shape
    n = wk.shape[2]
    n_la = w2.shape[2]
    full = lambda i: (0, 0)
    resident = lambda shape: pl.BlockSpec(shape, full, pipeline_mode=pl.Buffered(1))
    nv = n // MOBA_DH * MOBA_DV_AUG
    q_spec = pl.BlockSpec((tm // blk, n, blk), lambda i: (i, 0, 0))
    v_spec = pl.BlockSpec((tm // blk, nv, blk), lambda i: (i, 0, 0))
    q_shape = jax.ShapeDtypeStruct((t // blk, n, blk), BF16)
    v_shape = jax.ShapeDtypeStruct((t // blk, nv, blk), BF16)
    return pl.pallas_call(
        functools.partial(_moba_proj_kernel, scale=MOBA_DH ** -0.5 * LOG2E),
        grid=(t // tm,),
        in_specs=[pl.BlockSpec((tm, d), lambda i: (i, 0)),
                  _layer_weight(wq_t, li), _layer_weight(wk, li), _layer_weight(wv_t, li),
                  resident((MOBA_DH, 1)), resident((1, MOBA_DH)),
                  _layer_weight(w_lr, li), _layer_weight(w2, li), resident((1, n_la))],
        out_specs=[q_spec, pl.BlockSpec((tm, n), lambda i: (i, 0)), v_spec,
                   pl.BlockSpec((tm, n_la), lambda i: (i, 0)),
                   pl.BlockSpec((tm, n_la), lambda i: (i, 0))],
        out_shape=[q_shape, jax.ShapeDtypeStruct((t, n), BF16), v_shape,
                   jax.ShapeDtypeStruct((t, n_la), BF16), jax.ShapeDtypeStruct((t, n_la), BF16)],
        compiler_params=_cparams("parallel"),
        name="moba_qkv_proj",
    )(h, wq_t, wk, wv_t, gq.reshape(MOBA_DH, 1), gk.reshape(1, MOBA_DH), w_lr, w2, b)


def _gla_kernel(h_ref, lah_ref, lal_ref, wqk_ref, wv_ref, wr_ref, gn_ref, o_ref,
                st_ref, tri_ref, *, rb):
    n_k = GLA_HEADS * GLA_DK
    h = h_ref[...]
    qk_all = jnp.dot(h, wqk_ref[...], preferred_element_type=F32)

    c = GLA_CHUNK
    nc = rb // c
    shift = c.bit_length() - 1
    row = lax.broadcasted_iota(jnp.int32, (rb, rb), 0)
    col = lax.broadcasted_iota(jnp.int32, (rb, rb), 1)
    same_chunk = (row >> shift) == (col >> shift)
    causal = same_chunk & (row >= col)

    @pl.when(pl.program_id(1) == 0)
    def _():
        st_ref[...] = jnp.zeros_like(st_ref)
        tri_ref[...] = causal.astype(BF16)

    tri = tri_ref[...]
    heads = range(GLA_HEADS)
    kcs = [slice(hh * GLA_DK, (hh + 1) * GLA_DK) for hh in heads]
    vcs = [slice(hh * GLA_DV, (hh + 1) * GLA_DV) for hh in heads]
    b = (jnp.dot(tri, lah_ref[...], preferred_element_type=F32)
         + jnp.dot(tri, lal_ref[...], preferred_element_type=F32))
    v_all = jnp.dot(h, wv_ref[...], preferred_element_type=F32).astype(BF16)
    g_r = jnp.dot(h, wr_ref[...], preferred_element_type=F32)
    gr_all = g_r * _sigmoid(g_r)
    b_last = jnp.concatenate(
        [jnp.broadcast_to(b[ci * c + c - 1:ci * c + c, :], (c, b.shape[1])) for ci in range(nc)],
        axis=0)
    q_all = qk_all[:, :n_k]
    k_all = qk_all[:, n_k:]
    decay = jnp.exp(b)
    qd = (q_all * decay * (GLA_DK ** -0.5)).astype(BF16)
    kd = (k_all * jnp.exp(-b)).astype(BF16)
    kl = (k_all * jnp.exp(b_last - b)).astype(BF16)
    a = [lax.dot_general(qd[:, kcs[hh]], kd[:, kcs[hh]], NT_DIMS, preferred_element_type=F32)
         for hh in heads]
    chunks = [slice(ci * c, (ci + 1) * c) for ci in range(nc)]
    kv_t = [[lax.dot_general(v_all[rs, vcs[hh]], kl[rs, kcs[hh]], TN_DIMS,
                             preferred_element_type=F32) for hh in heads] for rs in chunks]
    o_intra = [jnp.dot(jnp.where(causal, a[hh], 0.0).astype(BF16), v_all[:, vcs[hh]],
                       preferred_element_type=F32) for hh in heads]
    st = [st_ref[hh] for hh in heads]
    o_inter = []
    for ci, rs in enumerate(chunks):
        o_inter.append([lax.dot_general(qd[rs, kcs[hh]], st[hh].astype(BF16), NT_DIMS,
                                        preferred_element_type=F32) for hh in heads])
        last = ci * c + c - 1
        st = [st[hh] * decay[last:last + 1, kcs[hh]] + kv_t[ci][hh] for hh in heads]
    for hh in heads:
        st_ref[hh] = st[hh]
    for ci, rs in enumerate(chunks):
        for hh in heads:
            o = o_intra[hh][rs] + o_inter[ci][hh]
            o_ref[rs, vcs[hh]] = (_rms(o, gn_ref[hh]) * gr_all[rs, vcs[hh]]).astype(o_ref.dtype)


def _gla(h, la_hi, la_lo, w_qk, w_v, w_r, gnorm, li, batch, seq, rb=GLA_ROW_BLOCK):
    t, d = h.shape
    nblk = seq // rb
    nh = GLA_HEADS
    rows = lambda bi, s: (bi * nblk + s, 0)
    return pl.pallas_call(
        functools.partial(_gla_kernel, rb=rb),
        grid=(batch, nblk),
        in_specs=[pl.BlockSpec((rb, d), rows), pl.BlockSpec((rb, la_hi.shape[1]), rows),
                  pl.BlockSpec((rb, la_lo.shape[1]), rows),
                  _layer_weight(w_qk, li), _layer_weight(w_v, li), _layer_weight(w_r, li),
                  pl.BlockSpec((nh, 1, GLA_DV), lambda bi, s: (0, 0, 0))],
        out_specs=pl.BlockSpec((rb, nh * GLA_DV), lambda bi, s: (bi * nblk + s, 0)),
        out_shape=jax.ShapeDtypeStruct((t, nh * GLA_DV), BF16),
        scratch_shapes=[pltpu.VMEM((nh, GLA_DV, GLA_DK), F32),
                        pltpu.VMEM((rb, rb), BF16)],
        compiler_params=_cparams("parallel", "arbitrary"),
        name="gla_branch",
    )(h, la_hi, la_lo, w_qk, w_v, w_r, gnorm.reshape(nh, 1, GLA_DV))


def _alibi_split(slope):
    c = slope * LOG2E
    c_hi = c.astype(BF16).astype(F32)
    return c_hi, c - c_hi


def _moba_build_keys(k, slope, kaug_ref, kmh_ref, kml_ref, nb):
    blk, dh = MOBA_BLOCK, MOBA_DH
    col = lax.broadcasted_iota(jnp.int32, (blk, dh), 1)
    r_key = lax.broadcasted_iota(jnp.int32, (blk, dh), 0).astype(F32)
    c_hi, c_lo = _alibi_split(slope[:, :dh])
    base = jnp.where((col == nb) | (col == nb + 4), c_hi, 0.0)
    base = jnp.where((col == nb + 1) | (col == nb + 5), c_lo, base)
    base = jnp.where((col == nb + 2) | (col == nb + 3), r_key, base)
    is_offset = (col == nb + 6) | (col == nb + 7)
    ones = jnp.ones((8, blk), BF16)
    sums = []
    for n in range(nb):
        kn = k[n * blk:(n + 1) * blk, :]
        kaug_ref[n * blk:(n + 1) * blk, :dh] = kn
        e = jnp.where(col == n, 1.0, jnp.where(is_offset, float(n * blk), base))
        kaug_ref[n * blk:(n + 1) * blk, dh:] = e.astype(BF16)
        sums.append(jnp.dot(ones, kn, preferred_element_type=F32)[:1])
    km = jnp.concatenate(sums, axis=0) * (1.0 / blk)
    km_hi = km.astype(BF16)
    kmh_ref[...] = km_hi
    kml_ref[...] = (km - km_hi.astype(F32)).astype(BF16)


def _moba_aug_queries(q_t, g_t, slope, first_blk, nb):
    blk, dh = MOBA_BLOCK, MOBA_DH
    qw = q_t.shape[1]
    shift = blk.bit_length() - 1
    nidx = lax.broadcasted_iota(jnp.int32, (nb, qw), 0)
    q_blk = first_blk + (lax.broadcasted_iota(jnp.int32, (nb, qw), 1) >> shift)
    valid = nidx < q_blk
    g = jnp.where(valid, g_t, -jnp.inf)
    rank = jnp.zeros((nb, qw), jnp.int32)
    for m in range(nb):
        gm = g[m:m + 1, :]
        beats = (gm > g) | ((gm == g) & (nidx > m))
        rank = rank + beats.astype(jnp.int32)
    selb = jnp.where(valid & (rank < MOBA_TOPK), 0.0, NEG_BIG)

    ridx = lax.broadcasted_iota(jnp.int32, (dh - nb, qw), 0) + nb
    lane = lax.broadcasted_iota(jnp.int32, (dh - nb, qw), 1)
    r_t = (lane & (blk - 1)).astype(F32)
    off = ((first_blk + (lane >> shift)) * blk).astype(F32)
    c_hi, c_lo = _alibi_split(slope)
    rest = jnp.where((ridx == nb) | (ridx == nb + 1), -r_t, 0.0)
    rest = jnp.where((ridx == nb + 2) | (ridx == nb + 6), c_hi, rest)
    rest = jnp.where((ridx == nb + 3) | (ridx == nb + 7), c_lo, rest)
    rest = jnp.where((ridx == nb + 4) | (ridx == nb + 5), -off, rest)
    x_t = jnp.concatenate([selb, rest], axis=0)
    return jnp.concatenate([q_t, x_t.astype(BF16)], axis=0)


def _moba_kernel(qt_ref, k_ref, vt_ref, slope_ref, o_ref, kaug_ref, qaug_ref, kmh_ref, kml_ref,
                 s0_ref, s1_ref, acc_ref, *, nb, hp, qb):
    blk, dh, dva = MOBA_BLOCK, MOBA_DH, MOBA_DV_AUG
    qw = qb * blk
    step = pl.program_id(2)
    first_blk = step * qb
    hs = [slice(hh * dh, (hh + 1) * dh) for hh in range(hp)]
    vs = [slice(hh * dva, (hh + 1) * dva) for hh in range(hp)]

    @pl.when(step == 0)
    def _():
        for hh in range(hp):
            _moba_build_keys(k_ref[:, hs[hh]], slope_ref[hh][:, :blk], kaug_ref.at[hh],
                             kmh_ref.at[hh], kml_ref.at[hh], nb)

    q_ts = [jnp.concatenate([qt_ref[c, hs[hh], :] for c in range(qb)], axis=1)
            for hh in range(hp)]
    lanes = [slice(c * blk, (c + 1) * blk) for c in range(qb)]
    blk_rows = [pl.ds(pl.multiple_of((first_blk + c) * blk, blk), blk) for c in range(qb)]
    owns = [[jnp.dot(k_ref[blk_rows[c], hs[hh]], q_ts[hh][:, lanes[c]],
                     preferred_element_type=F32) for c in range(qb)]
            for hh in range(hp)]
    gates = [jnp.dot(kmh_ref[hh], q_ts[hh], preferred_element_type=F32)
             + jnp.dot(kml_ref[hh], q_ts[hh], preferred_element_type=F32)
             for hh in range(hp)]
    for hh in range(hp):
        qaug_ref[hh] = _moba_aug_queries(q_ts[hh], gates[hh], slope_ref[hh], first_blk, nb)

    def past_scores(hh, j, qcols=slice(None)):
        rows = pl.ds(pl.multiple_of(j * blk, blk), blk)
        s = jnp.dot(kaug_ref[hh, rows, :], qaug_ref[hh, :, qcols], preferred_element_type=F32)
        return s, jnp.max(s, axis=0, keepdims=True)

    def update(hh, s, s_max, j, m, qcols=slice(None)):
        m_new = jnp.maximum(m, s_max)
        alpha = jnp.exp2(m - m_new)
        p = jnp.exp2((s - m_new).astype(BF16))
        pv = jnp.dot(vt_ref[j, vs[hh], :], p, preferred_element_type=F32)
        acc_ref[hh, :, qcols] = alpha * acc_ref[hh, :, qcols] + pv
        return m_new

    kk = lax.broadcasted_iota(jnp.int32, (blk, blk), 0)
    qq = lax.broadcasted_iota(jnp.int32, (blk, blk), 1)
    dist = (qq - kk).astype(F32)
    ms = []
    for hh in range(hp):
        acc_ref[hh] = jnp.zeros((dva, qw), F32)
        c_log2 = slope_ref[hh][:, :blk] * LOG2E
        m_blocks = []
        for c in range(qb):
            own = jnp.where(qq >= kk, owns[hh][c] - c_log2 * dist, NEG_BIG)
            m_blocks.append(update(hh, own, jnp.max(own, axis=0, keepdims=True), first_blk + c,
                                   jnp.full((1, blk), -jnp.inf, F32), lanes[c]))
        ms.append(m_blocks)
    max0 = []
    for hh in range(hp):
        s, s_max = past_scores(hh, 0)
        s0_ref[hh] = s
        max0.append(s_max)
    carry = []
    for hh in range(hp):
        for c in range(qb - 1):
            later = slice((c + 1) * blk, qw)
            s, s_max = past_scores(hh, first_blk + c, later)
            m_later = update(hh, s, s_max, first_blk + c, jnp.concatenate(ms[hh][c + 1:], axis=1),
                             later)
            ms[hh][c + 1:] = [m_later[:, n * blk:(n + 1) * blk] for n in range(qb - 1 - c)]
        carry.append((jnp.concatenate(ms[hh], axis=1), max0[hh]))

    def body(jj, carry):
        j0 = 2 * jj
        j2 = jnp.minimum(j0 + 2, nb - 1)
        ms = [c[0] for c in carry]
        max0 = [c[1] for c in carry]
        max1 = []
        for hh in range(hp):
            s, s_max = past_scores(hh, j0 + 1)
            s1_ref[hh] = s
            max1.append(s_max)
        for hh in range(hp):
            ms[hh] = update(hh, s0_ref[hh], max0[hh], j0, ms[hh])
        for hh in range(hp):
            s, max0[hh] = past_scores(hh, j2)
            s0_ref[hh] = s
        for hh in range(hp):
            ms[hh] = update(hh, s1_ref[hh], max1[hh], j0 + 1, ms[hh])
        return tuple(zip(ms, max0))

    def last_block(_, carry):
        return tuple((update(hh, s0_ref[hh], carry[hh][1], first_blk - 1, carry[hh][0]),
                      carry[hh][1]) for hh in range(hp))

    def trips(n_pairs):
        def many(kk, c):
            for r in range(n_pairs):
                c = body(n_pairs * kk + r, c)
            return c
        return many

    pairs_done = 0
    carry = tuple(carry)
    for n_pairs in (4, 2, 1):
        n_trips = (first_blk // 2 - pairs_done) // n_pairs
        start = pairs_done // n_pairs
        carry = lax.fori_loop(start, start + n_trips, trips(n_pairs), carry)
        pairs_done = pairs_done + n_trips * n_pairs
    lax.fori_loop(0, first_blk % 2, last_block, carry)
    for hh in range(hp):
        acc = acc_ref[hh]
        o_ref[:, hs[hh]] = (acc[:dh] / acc[dh:dh + 1]).T.astype(o_ref.dtype)


def _moba(q_t, k, v_t, slopes, batch, seq, hp=MOBA_HEADS_PER_STEP, qb=MOBA_QBLOCKS_PER_STEP):
    t = k.shape[0]
    h, dh, dva, blk = MOBA_HEADS, MOBA_DH, MOBA_DV_AUG, MOBA_BLOCK
    nb = seq // blk
    ns = nb // qb
    qw = qb * blk
    hg = h // hp
    return pl.pallas_call(
        functools.partial(_moba_kernel, nb=nb, hp=hp, qb=qb),
        grid=(batch, hg, ns),
        in_specs=[pl.BlockSpec((qb, hp * dh, blk), lambda b, g, i: (b * ns + i, g, 0)),
                  pl.BlockSpec((seq, hp * dh), lambda b, g, i: (b, g)),
                  pl.BlockSpec((nb, hp * dva, blk), lambda b, g, i: (b, g, 0)),
                  pl.BlockSpec((hp, 1, qw), lambda b, g, i: (g, 0, 0))],
        out_specs=pl.BlockSpec((qw, hp * dh), lambda b, g, i: (b * ns + i, g)),
        out_shape=jax.ShapeDtypeStruct((t, h * dh), BF16),
        scratch_shapes=[pltpu.VMEM((hp, seq, 2 * dh), BF16),
                        pltpu.VMEM((hp, 2 * dh, qw), BF16),
                        pltpu.VMEM((hp, nb, dh), BF16),
                        pltpu.VMEM((hp, nb, dh), BF16),
                        pltpu.VMEM((hp, blk, qw), F32),
                        pltpu.VMEM((hp, blk, qw), F32),
                        pltpu.VMEM((hp, dva, qw), F32)],
        compiler_params=_cparams("parallel", "parallel", "arbitrary"),
        name="moba_attn",
    )(q_t, k, v_t, slopes)


def _merge_kernel(x_ref, h_ref, oa_ref, ob_ref, wg_ref, wa_ref, wb_ref, wo_ref, o_ref):
    d = x_ref.shape[1]
    gates = _sigmoid(jnp.dot(h_ref[...], wg_ref[...], preferred_element_type=F32))
    ya = jnp.dot(oa_ref[...], wa_ref[...], preferred_element_type=F32)
    yb = jnp.dot(ob_ref[...], wb_ref[...], preferred_element_type=F32)
    y = gates[:, :d] * ya + gates[:, d:] * yb
    o_ref[...] = x_ref[...] + jnp.dot(y.astype(BF16), wo_ref[...], preferred_element_type=F32)


def _merge(x, h, oa, ob, wg, wa, wb, wo, li, tm=512):
    t, d = x.shape
    row = lambda i: (i, 0)
    return pl.pallas_call(
        _merge_kernel,
        grid=(t // tm,),
        in_specs=[pl.BlockSpec((tm, d), row), pl.BlockSpec((tm, d), row),
                  pl.BlockSpec((tm, d), row), pl.BlockSpec((tm, d), row),
                  _layer_weight(wg, li), _layer_weight(wa, li),
                  _layer_weight(wb, li), _layer_weight(wo, li)],
        out_specs=pl.BlockSpec((tm, d), row),
        out_shape=jax.ShapeDtypeStruct((t, d), F32),
        compiler_params=_cparams("parallel"),
        name="merge_out_proj",
    )(x, h, oa, ob, wg, wa, wb, wo)


def _mlp_ple_kernel(x_ref, p_ref, gm_ref, wu_ref, wd_ref, gp_ref, wg_ref, wp_ref, gn_ref,
                    o_ref, *maybe_h_ref, nchunk):
    x = x_ref[...]
    h2 = _rms(x, gm_ref[...]).astype(BF16)
    tf = wu_ref.shape[1] // nchunk
    acc = x
    for c in range(nchunk):
        up = jnp.dot(h2, wu_ref[:, c * tf:(c + 1) * tf], preferred_element_type=F32)
        act = jnp.square(jnp.maximum(up, 0.0)).astype(BF16)
        acc = acc + jnp.dot(act, wd_ref[c * tf:(c + 1) * tf, :], preferred_element_type=F32)
    hn = _rms(acc, gp_ref[...]).astype(BF16)
    gate = _sigmoid(jnp.dot(hn, wg_ref[...], preferred_element_type=F32))
    e = jnp.dot(p_ref[...].astype(BF16), wp_ref[...], preferred_element_type=F32)
    xo = acc + gate * e
    o_ref[...] = xo
    if maybe_h_ref:
        maybe_h_ref[0][...] = _rms(xo, gn_ref[...]).astype(BF16)


def _mlp_ple(x, p, g_mlp, wu, wd, g_ple, wg, wp, g_next, li, tm=512, nchunk=4):
    t, d = x.shape
    pd = p.shape[2]
    row = lambda i: (i, 0)
    full = lambda i: (0, 0)
    resident = lambda shape: pl.BlockSpec(shape, full, pipeline_mode=pl.Buffered(1))
    emit_next = g_next is not None
    out_shape = [jax.ShapeDtypeStruct((t, d), F32)]
    out_specs = [pl.BlockSpec((tm, d), row)]
    if emit_next:
        out_shape.append(jax.ShapeDtypeStruct((t, d), BF16))
        out_specs.append(pl.BlockSpec((tm, d), row))
    gn = (g_next if emit_next else g_ple).reshape(1, d)
    res = pl.pallas_call(
        functools.partial(_mlp_ple_kernel, nchunk=nchunk),
        grid=(t // tm,),
        in_specs=[pl.BlockSpec((tm, d), row), pl.BlockSpec((None, tm, pd), lambda i: (li, i, 0)),
                  resident((1, d)), _layer_weight(wu, li), _layer_weight(wd, li),
                  resident((1, d)), _layer_weight(wg, li), _layer_weight(wp, li),
                  resident((1, d))],
        out_specs=out_specs,
        out_shape=out_shape,
        compiler_params=_cparams("parallel"),
        name="mlp_ple",
    )(x, p, g_mlp.reshape(1, d), wu, wd, g_ple.reshape(1, d), wg, wp, gn)
    return (res[0], res[1]) if emit_next else (res[0], None)


def kernel(x, p, norm_mix, w_in, gla_gate_w2, gla_gate_b, gla_out_norm, moba_q_norm,
           moba_k_norm, w_branch_a, w_branch_b, w_out, norm_mlp, w_up, w_down,
           norm_ple, w_ple_gate, w_ple):
    batch, seq, d = x.shape
    depth = w_in.shape[0]
    t = batch * seq
    x = x.reshape(t, d)

    slopes = 2.0 ** (-8.0 * jnp.arange(1, MOBA_HEADS + 1, dtype=F32) / MOBA_HEADS)
    slopes = jnp.broadcast_to(slopes[:, None, None],
                              (MOBA_HEADS, 1, MOBA_QBLOCKS_PER_STEP * MOBA_BLOCK))

    w_gqk, w_gv, w_lr, w_gr, w_mq_t, w_mk, w_mv_t, w_gate = _split_w_in(w_in)
    w2 = jnp.pad(gla_gate_w2, ((0, 0), (0, LANES - GLA_GATE_RANK), (0, 0))).astype(BF16)
    w_a, w_b, w_o = (w.astype(BF16) for w in (w_branch_a, w_branch_b, w_out))
    w_u, w_d, w_pg, w_pe = (w.astype(BF16) for w in (w_up, w_down, w_ple_gate, w_ple))
    p = p.reshape(depth, t, -1)

    h = _norm_cast(x, norm_mix[0])
    for li in range(depth):
        mq_t, mk, mv_t, la_hi, la_lo = _moba_proj(
            h, w_mq_t, w_mk, w_mv_t, moba_q_norm[li], moba_k_norm[li], w_lr, w2,
            gla_gate_b[li].reshape(1, -1), li)

        oa = _gla(h, la_hi, la_lo, w_gqk, w_gv, w_gr, gla_out_norm[li], li, batch, seq)
        ob = _moba(mq_t, mk, mv_t, slopes, batch, seq)

        x = _merge(x, h, oa, ob, w_gate, w_a, w_b, w_o, li)
        g_next = norm_mix[li + 1] if li + 1 < depth else None
        x, h = _mlp_ple(x, p, norm_mlp[li], w_u, w_d, norm_ple[li], w_pg, w_pe, g_next, li)
    return x.reshape(batch, seq, d)
```

```python
import functools

import jax
import jax.numpy as jnp
from jax import lax
from jax.experimental import pallas as pl
from jax.experimental.pallas import tpu as pltpu

F32 = jnp.float32
BF16 = jnp.bfloat16

EPS = 1e-6
GLA_HEADS = 4
GLA_DK = 128
GLA_DV = 256
GLA_GATE_RANK = 16
GLA_GATE_TAU = 16.0
GLA_CHUNK = 64
GLA_ROW_BLOCK = 256
MOBA_HEADS = 8
MOBA_DH = 128
MOBA_BLOCK = 256
MOBA_TOPK = 3
MOBA_HEADS_PER_STEP = 4
MOBA_QBLOCKS_PER_STEP = 1
MOBA_ONES_ROWS = 16
MOBA_DV_AUG = MOBA_DH + MOBA_ONES_ROWS
LOG2E = 1.4426950408889634

LANES = 128
VMEM_LIMIT = 48 * 1024 * 1024
NEG_BIG = -1e30

NT_DIMS = (((1,), (1,)), ((), ()))
TN_DIMS = (((0,), (0,)), ((), ()))


def _cparams(*sem):
    return pltpu.CompilerParams(dimension_semantics=sem, vmem_limit_bytes=VMEM_LIMIT)


def _layer_weight(stacked, li):
    return pl.BlockSpec((None,) + stacked.shape[1:], lambda *_: (li, 0, 0),
                        pipeline_mode=pl.Buffered(1))


def _rms(x, g):
    return x * lax.rsqrt(jnp.mean(x * x, axis=-1, keepdims=True) + EPS) * g


def _sigmoid(x):
    return 1.0 / (1.0 + jnp.exp(-x))


def _norm_kernel(x_ref, g_ref, o_ref):
    o_ref[...] = _rms(x_ref[...], g_ref[...]).astype(o_ref.dtype)


def _norm_cast(x, g, tm=1024):
    t, d = x.shape
    return pl.pallas_call(
        _norm_kernel,
        grid=(t // tm,),
        in_specs=[pl.BlockSpec((tm, d), lambda i: (i, 0)),
                  pl.BlockSpec((1, d), lambda i: (0, 0))],
        out_specs=pl.BlockSpec((tm, d), lambda i: (i, 0)),
        out_shape=jax.ShapeDtypeStruct((t, d), BF16),
        compiler_params=_cparams("parallel"),
        name="norm_cast",
    )(x, g.reshape(1, d))


def _split_w_in_kernel(wt_ref, gqk_ref, gv_ref, lr_ref, gr_ref, mqt_ref, mk_ref, mvt_ref, gate_ref):
    offs = [0]
    for ref in (gqk_ref, gv_ref):
        offs.append(offs[-1] + ref.shape[1])
    offs.append(offs[-1] + GLA_GATE_RANK)
    for n in (gr_ref.shape[1], mqt_ref.shape[0], mk_ref.shape[1], mvt_ref.shape[0],
              gate_ref.shape[1]):
        offs.append(offs[-1] + n)
    piece_t = lambda i: wt_ref[offs[i]:offs[i + 1], :]
    gqk_ref[...] = piece_t(0).T.astype(BF16)
    gv_ref[...] = piece_t(1).T.astype(BF16)
    lr_t = wt_ref[offs[2]:offs[2] + LANES, :].T
    lane = lax.broadcasted_iota(jnp.int32, lr_t.shape, 1)
    lr_ref[...] = jnp.where(lane < GLA_GATE_RANK, lr_t, 0.0).astype(BF16)
    gr_ref[...] = piece_t(3).T.astype(BF16)
    mqt_ref[...] = piece_t(4).astype(BF16)
    mk_ref[...] = piece_t(5).T.astype(BF16)
    mvt_ref[...] = piece_t(6).astype(BF16)
    gate_ref[...] = piece_t(7).T.astype(BF16)


def _split_w_in(w_in, tc=256):
    depth, d, n_in = w_in.shape
    n_gqk = 2 * GLA_HEADS * GLA_DK
    n_gv = GLA_HEADS * GLA_DV
    n_m = MOBA_HEADS * MOBA_DH
    assert n_in == n_gqk + n_gv + GLA_GATE_RANK + n_gv + 3 * n_m + 2 * d
    rows = lambda l, i: (l, i, 0)
    cols = lambda l, i: (l, 0, i)
    row_major = lambda n: ((depth, d, n), pl.BlockSpec((None, tc, n), rows))
    transposed = lambda n: ((depth, n, d), pl.BlockSpec((None, n, tc), cols))
    outs = [row_major(n_gqk), row_major(n_gv), row_major(LANES), row_major(n_gv),
            transposed(n_m), row_major(n_m), transposed(n_m), row_major(2 * d)]
    return pl.pallas_call(
        _split_w_in_kernel,
        grid=(depth, d // tc),
        in_specs=[pl.BlockSpec((None, n_in, tc), cols)],
        out_specs=[spec for _, spec in outs],
        out_shape=[jax.ShapeDtypeStruct(shape, BF16) for shape, _ in outs],
        compiler_params=_cparams("parallel", "parallel"),
        name="split_w_in",
    )(jnp.swapaxes(w_in, 1, 2))


def _moba_proj_kernel(h_ref, wqt_ref, wk_ref, wvt_ref, gq_ref, gk_ref, wlr_ref, w2_ref, b_ref,
                      qt_ref, k_ref, vt_ref, lah_ref, lal_ref, *, scale):
    dh = MOBA_DH
    blk = qt_ref.shape[2]
    h = h_ref[...]
    nheads = k_ref.shape[1] // dh

    lr = jnp.dot(h, wlr_ref[...], preferred_element_type=F32)
    q_t = lax.dot_general(wqt_ref[...], h, NT_DIMS, preferred_element_type=F32)
    gq = gq_ref[...] * scale
    segs = []
    for hh in range(nheads):
        seg = q_t[hh * dh:(hh + 1) * dh, :]
        ms = jnp.mean(seg * seg, axis=0, keepdims=True)
        segs.append((seg * lax.rsqrt(ms + EPS) * gq).astype(qt_ref.dtype))
    q_t = jnp.concatenate(segs, axis=0)

    z = jnp.dot(lr.astype(BF16), w2_ref[...], preferred_element_type=F32) + b_ref[...]
    k = jnp.dot(h, wk_ref[...], preferred_element_type=F32)
    gk = gk_ref[...]
    for hh in range(nheads):
        k_ref[:, hh * dh:(hh + 1) * dh] = _rms(k[:, hh * dh:(hh + 1) * dh], gk).astype(k_ref.dtype)

    v_t = lax.dot_general(wvt_ref[...], h, NT_DIMS, preferred_element_type=F32)
    ones = jnp.ones((MOBA_ONES_ROWS, v_t.shape[1]), vt_ref.dtype)
    parts = []
    for hh in range(nheads):
        parts += [v_t[hh * dh:(hh + 1) * dh, :].astype(vt_ref.dtype), ones]
    v_t = jnp.concatenate(parts, axis=0)
    for c in range(qt_ref.shape[0]):
        qt_ref[c] = q_t[:, c * blk:(c + 1) * blk]
        vt_ref[c] = v_t[:, c * blk:(c + 1) * blk]

    log_a = (jnp.minimum(z, 0.0) - jnp.log(1.0 + jnp.exp(-jnp.abs(z)))) * (1.0 / GLA_GATE_TAU)
    la_hi = log_a.astype(BF16)
    lah_ref[...] = la_hi
    lal_ref[...] = (log_a - la_hi.astype(F32)).astype(BF16)


def _moba_proj(h, wq_t, wk, wv_t, gq, gk, w_lr, w2, b, li, tm=1024, blk=MOBA_BLOCK):
    t, d = h.shape
    n = wk.shape[2]
    n_la = w2.shape[2]
    full = lambda i: (0, 0)
    resident = lambda shape: pl.BlockSpec(shape, full, pipeline_mode=pl.Buffered(1))
    nv = n // MOBA_DH * MOBA_DV_AUG
    q_spec = pl.BlockSpec((tm // blk, n, blk), lambda i: (i, 0, 0))
    v_spec = pl.BlockSpec((tm // blk, nv, blk), lambda i: (i, 0, 0))
    q_shape = jax.ShapeDtypeStruct((t // blk, n, blk), BF16)
    v_shape = jax.ShapeDtypeStruct((t // blk, nv, blk), BF16)
    return pl.pallas_call(
        functools.partial(_moba_proj_kernel, scale=MOBA_DH ** -0.5 * LOG2E),
        grid=(t // tm,),
        in_specs=[pl.BlockSpec((tm, d), lambda i: (i, 0)),
                  _layer_weight(wq_t, li), _layer_weight(wk, li), _layer_weight(wv_t, li),
                  resident((MOBA_DH, 1)), resident((1, MOBA_DH)),
                  _layer_weight(w_lr, li), _layer_weight(w2, li), resident((1, n_la))],
        out_specs=[q_spec, pl.BlockSpec((tm, n), lambda i: (i, 0)), v_spec,
                   pl.BlockSpec((tm, n_la), lambda i: (i, 0)),
                   pl.BlockSpec((tm, n_la), lambda i: (i, 0))],
        out_shape=[q_shape, jax.ShapeDtypeStruct((t, n), BF16), v_shape,
                   jax.ShapeDtypeStruct((t, n_la), BF16), jax.ShapeDtypeStruct((t, n_la), BF16)],
        compiler_params=_cparams("parallel"),
        name="moba_qkv_proj",
    )(h, wq_t, wk, wv_t, gq.reshape(MOBA_DH, 1), gk.reshape(1, MOBA_DH), w_lr, w2, b)


def _gla_kernel(h_ref, lah_ref, lal_ref, wqk_ref, wv_ref, wr_ref, gn_ref, o_ref,
                st_ref, tri_ref, *, rb):
    n_k = GLA_HEADS * GLA_DK
    h = h_ref[...]
    qk_all = jnp.dot(h, wqk_ref[...], preferred_element_type=F32)

    c = GLA_CHUNK
    nc = rb // c
    shift = c.bit_length() - 1
    row = lax.broadcasted_iota(jnp.int32, (rb, rb), 0)
    col = lax.broadcasted_iota(jnp.int32, (rb, rb), 1)
    same_chunk = (row >> shift) == (col >> shift)
    causal = same_chunk & (row >= col)

    @pl.when(pl.program_id(1) == 0)
    def _():
        st_ref[...] = jnp.zeros_like(st_ref)
        tri_ref[...] = causal.astype(BF16)

    tri = tri_ref[...]
    heads = range(GLA_HEADS)
    kcs = [slice(hh * GLA_DK, (hh + 1) * GLA_DK) for hh in heads]
    vcs = [slice(hh * GLA_DV, (hh + 1) * GLA_DV) for hh in heads]
    b = (jnp.dot(tri, lah_ref[...], preferred_element_type=F32)
         + jnp.dot(tri, lal_ref[...], preferred_element_type=F32))
    v_all = jnp.dot(h, wv_ref[...], preferred_element_type=F32).astype(BF16)
    g_r = jnp.dot(h, wr_ref[...], preferred_element_type=F32)
    gr_all = g_r * _sigmoid(g_r)
    b_last = jnp.concatenate(
        [jnp.broadcast_to(b[ci * c + c - 1:ci * c + c, :], (c, b.shape[1])) for ci in range(nc)],
        axis=0)
    q_all = qk_all[:, :n_k]
    k_all = qk_all[:, n_k:]
    decay = jnp.exp(b)
    qd = (q_all * decay * (GLA_DK ** -0.5)).astype(BF16)
    kd = (k_all * jnp.exp(-b)).astype(BF16)
    kl = (k_all * jnp.exp(b_last - b)).astype(BF16)
    a = [lax.dot_general(qd[:, kcs[hh]], kd[:, kcs[hh]], NT_DIMS, preferred_element_type=F32)
         for hh in heads]
    chunks = [slice(ci * c, (ci + 1) * c) for ci in range(nc)]
    kv_t = [[lax.dot_general(v_all[rs, vcs[hh]], kl[rs, kcs[hh]], TN_DIMS,
                             preferred_element_type=F32) for hh in heads] for rs in chunks]
    o_intra = [jnp.dot(jnp.where(causal, a[hh], 0.0).astype(BF16), v_all[:, vcs[hh]],
                       preferred_element_type=F32) for hh in heads]
    st = [st_ref[hh] for hh in heads]
    o_inter = []
    for ci, rs in enumerate(chunks):
        o_inter.append([lax.dot_general(qd[rs, kcs[hh]], st[hh].astype(BF16), NT_DIMS,
                                        preferred_element_type=F32) for hh in heads])
        last = ci * c + c - 1
        st = [st[hh] * decay[last:last + 1, kcs[hh]] + kv_t[ci][hh] for hh in heads]
    for hh in heads:
        st_ref[hh] = st[hh]
    for ci, rs in enumerate(chunks):
        for hh in heads:
            o = o_intra[hh][rs] + o_inter[ci][hh]
            o_ref[rs, vcs[hh]] = (_rms(o, gn_ref[hh]) * gr_all[rs, vcs[hh]]).astype(o_ref.dtype)


def _gla(h, la_hi, la_lo, w_qk, w_v, w_r, gnorm, li, batch, seq, rb=GLA_ROW_BLOCK):
    t, d = h.shape
    nblk = seq // rb
    nh = GLA_HEADS
    rows = lambda bi, s: (bi * nblk + s, 0)
    return pl.pallas_call(
        functools.partial(_gla_kernel, rb=rb),
        grid=(batch, nblk),
        in_specs=[pl.BlockSpec((rb, d), rows), pl.BlockSpec((rb, la_hi.shape[1]), rows),
                  pl.BlockSpec((rb, la_lo.shape[1]), rows),
                  _layer_weight(w_qk, li), _layer_weight(w_v, li), _layer_weight(w_r, li),
                  pl.BlockSpec((nh, 1, GLA_DV), lambda bi, s: (0, 0, 0))],
        out_specs=pl.BlockSpec((rb, nh * GLA_DV), lambda bi, s: (bi * nblk + s, 0)),
        out_shape=jax.ShapeDtypeStruct((t, nh * GLA_DV), BF16),
        scratch_shapes=[pltpu.VMEM((nh, GLA_DV, GLA_DK), F32),
                        pltpu.VMEM((rb, rb), BF16)],
        compiler_params=_cparams("parallel", "arbitrary"),
        name="gla_branch",
    )(h, la_hi, la_lo, w_qk, w_v, w_r, gnorm.reshape(nh, 1, GLA_DV))


def _alibi_split(slope):
    c = slope * LOG2E
    c_hi = c.astype(BF16).astype(F32)
    return c_hi, c - c_hi


def _moba_build_keys(k, slope, kaug_ref, kmh_ref, kml_ref, nb):
    blk, dh = MOBA_BLOCK, MOBA_DH
    col = lax.broadcasted_iota(jnp.int32, (blk, dh), 1)
    r_key = lax.broadcasted_iota(jnp.int32, (blk, dh), 0).astype(F32)
    c_hi, c_lo = _alibi_split(slope[:, :dh])
    base = jnp.where((col == nb) | (col == nb + 4), c_hi, 0.0)
    base = jnp.where((col == nb + 1) | (col == nb + 5), c_lo, base)
    base = jnp.where((col == nb + 2) | (col == nb + 3), r_key, base)
    is_offset = (col == nb + 6) | (col == nb + 7)
    ones = jnp.ones((8, blk), BF16)
    sums = []
    for n in range(nb):
        kn = k[n * blk:(n + 1) * blk, :]
        kaug_ref[n * blk:(n + 1) * blk, :dh] = kn
        e = jnp.where(col == n, 1.0, jnp.where(is_offset, float(n * blk), base))
        kaug_ref[n * blk:(n + 1) * blk, dh:] = e.astype(BF16)
        sums.append(jnp.dot(ones, kn, preferred_element_type=F32)[:1])
    km = jnp.concatenate(sums, axis=0) * (1.0 / blk)
    km_hi = km.astype(BF16)
    kmh_ref[...] = km_hi
    kml_ref[...] = (km - km_hi.astype(F32)).astype(BF16)


def _moba_aug_queries(q_t, g_t, slope, first_blk, nb):
    blk, dh = MOBA_BLOCK, MOBA_DH
    qw = q_t.shape[1]
    shift = blk.bit_length() - 1
    nidx = lax.broadcasted_iota(jnp.int32, (nb, qw), 0)
    q_blk = first_blk + (lax.broadcasted_iota(jnp.int32, (nb, qw), 1) >> shift)
    valid = nidx < q_blk
    g = jnp.where(valid, g_t, -jnp.inf)
    rank = jnp.zeros((nb, qw), jnp.int32)
    for m in range(nb):
        gm = g[m:m + 1, :]
        beats = (gm > g) | ((gm == g) & (nidx > m))
        rank = rank + beats.astype(jnp.int32)
    selb = jnp.where(valid & (rank < MOBA_TOPK), 0.0, NEG_BIG)

    ridx = lax.broadcasted_iota(jnp.int32, (dh - nb, qw), 0) + nb
    lane = lax.broadcasted_iota(jnp.int32, (dh - nb, qw), 1)
    r_t = (lane & (blk - 1)).astype(F32)
    off = ((first_blk + (lane >> shift)) * blk).astype(F32)
    c_hi, c_lo = _alibi_split(slope)
    rest = jnp.where((ridx == nb) | (ridx == nb + 1), -r_t, 0.0)
    rest = jnp.where((ridx == nb + 2) | (ridx == nb + 6), c_hi, rest)
    rest = jnp.where((ridx == nb + 3) | (ridx == nb + 7), c_lo, rest)
    rest = jnp.where((ridx == nb + 4) | (ridx == nb + 5), -off, rest)
    x_t = jnp.concatenate([selb, rest], axis=0)
    return jnp.concatenate([q_t, x_t.astype(BF16)], axis=0)


def _moba_kernel(qt_ref, k_ref, vt_ref, slope_ref, o_ref, kaug_ref, qaug_ref, kmh_ref, kml_ref,
                 s0_ref, s1_ref, acc_ref, *, nb, hp, qb):
    blk, dh, dva = MOBA_BLOCK, MOBA_DH, MOBA_DV_AUG
    qw = qb * blk
    step = pl.program_id(2)
    first_blk = step * qb
    hs = [slice(hh * dh, (hh + 1) * dh) for hh in range(hp)]
    vs = [slice(hh * dva, (hh + 1) * dva) for hh in range(hp)]

    @pl.when(step == 0)
    def _():
        for hh in range(hp):
            _moba_build_keys(k_ref[:, hs[hh]], slope_ref[hh][:, :blk], kaug_ref.at[hh],
                             kmh_ref.at[hh], kml_ref.at[hh], nb)

    q_ts = [jnp.concatenate([qt_ref[c, hs[hh], :] for c in range(qb)], axis=1)
            for hh in range(hp)]
    lanes = [slice(c * blk, (c + 1) * blk) for c in range(qb)]
    blk_rows = [pl.ds(pl.multiple_of((first_blk + c) * blk, blk), blk) for c in range(qb)]
    owns = [[jnp.dot(k_ref[blk_rows[c], hs[hh]], q_ts[hh][:, lanes[c]],
                     preferred_element_type=F32) for c in range(qb)]
            for hh in range(hp)]
    gates = [jnp.dot(kmh_ref[hh], q_ts[hh], preferred_element_type=F32)
             + jnp.dot(kml_ref[hh], q_ts[hh], preferred_element_type=F32)
             for hh in range(hp)]
    for hh in range(hp):
        qaug_ref[hh] = _moba_aug_queries(q_ts[hh], gates[hh], slope_ref[hh], first_blk, nb)

    def past_scores(hh, j, qcols=slice(None)):
        rows = pl.ds(pl.multiple_of(j * blk, blk), blk)
        s = jnp.dot(kaug_ref[hh, rows, :], qaug_ref[hh, :, qcols], preferred_element_type=F32)
        return s, jnp.max(s, axis=0, keepdims=True)

    def update(hh, s, s_max, j, m, qcols=slice(None)):
        m_new = jnp.maximum(m, s_max)
        alpha = jnp.exp2(m - m_new)
        p = jnp.exp2((s - m_new).astype(BF16))
        pv = jnp.dot(vt_ref[j, vs[hh], :], p, preferred_element_type=F32)
        acc_ref[hh, :, qcols] = alpha * acc_ref[hh, :, qcols] + pv
        return m_new

    kk = lax.broadcasted_iota(jnp.int32, (blk, blk), 0)
    qq = lax.broadcasted_iota(jnp.int32, (blk, blk), 1)
    dist = (qq - kk).astype(F32)
    ms = []
    for hh in range(hp):
        acc_ref[hh] = jnp.zeros((dva, qw), F32)
        c_log2 = slope_ref[hh][:, :blk] * LOG2E
        m_blocks = []
        for c in range(qb):
            own = jnp.where(qq >= kk, owns[hh][c] - c_log2 * dist, NEG_BIG)
            m_blocks.append(update(hh, own, jnp.max(own, axis=0, keepdims=True), first_blk + c,
                                   jnp.full((1, blk), -jnp.inf, F32), lanes[c]))
        ms.append(m_blocks)
    max0 = []
    for hh in range(hp):
        s, s_max = past_scores(hh, 0)
        s0_ref[hh] = s
        max0.append(s_max)
    carry = []
    for hh in range(hp):
        for c in range(qb - 1):
            later = slice((c + 1) * blk, qw)
            s, s_max = past_scores(hh, first_blk + c, later)
            m_later = update(hh, s, s_max, first_blk + c, jnp.concatenate(ms[hh][c + 1:], axis=1),
                             later)
            ms[hh][c + 1:] = [m_later[:, n * blk:(n + 1) * blk] for n in range(qb - 1 - c)]
        carry.append((jnp.concatenate(ms[hh], axis=1), max0[hh]))

    def body(jj, carry):
        j0 = 2 * jj
        j2 = jnp.minimum(j0 + 2, nb - 1)
        ms = [c[0] for c in carry]
        max0 = [c[1] for c in carry]
        max1 = []
        for hh in range(hp):
            s, s_max = past_scores(hh, j0 + 1)
            s1_ref[hh] = s
            max1.append(s_max)
        for hh in range(hp):
            ms[hh] = update(hh, s0_ref[hh], max0[hh], j0, ms[hh])
        for hh in range(hp):
            s, max0[hh] = past_scores(hh, j2)
            s0_ref[hh] = s
        for hh in range(hp):
            ms[hh] = update(hh, s1_ref[hh], max1[hh], j0 + 1, ms[hh])
        return tuple(zip(ms, max0))

    def last_block(_, carry):
        return tuple((update(hh, s0_ref[hh], carry[hh][1], first_blk - 1, carry[hh][0]),
                      carry[hh][1]) for hh in range(hp))

    def trips(n_pairs):
        def many(kk, c):
            for r in range(n_pairs):
                c = body(n_pairs * kk + r, c)
            return c
        return many

    pairs_done = 0
    carry = tuple(carry)
    for n_pairs in (4, 2, 1):
        n_trips = (first_blk // 2 - pairs_done) // n_pairs
        start = pairs_done // n_pairs
        carry = lax.fori_loop(start, start + n_trips, trips(n_pairs), carry)
        pairs_done = pairs_done + n_trips * n_pairs
    lax.fori_loop(0, first_blk % 2, last_block, carry)
    for hh in range(hp):
        acc = acc_ref[hh]
        o_ref[:, hs[hh]] = (acc[:dh] / acc[dh:dh + 1]).T.astype(o_ref.dtype)


def _moba(q_t, k, v_t, slopes, batch, seq, hp=MOBA_HEADS_PER_STEP, qb=MOBA_QBLOCKS_PER_STEP):
    t = k.shape[0]
    h, dh, dva, blk = MOBA_HEADS, MOBA_DH, MOBA_DV_AUG, MOBA_BLOCK
    nb = seq // blk
    ns = nb // qb
    qw = qb * blk
    hg = h // hp
    return pl.pallas_call(
        functools.partial(_moba_kernel, nb=nb, hp=hp, qb=qb),
        grid=(batch, hg, ns),
        in_specs=[pl.BlockSpec((qb, hp * dh, blk), lambda b, g, i: (b * ns + i, g, 0)),
                  pl.BlockSpec((seq, hp * dh), lambda b, g, i: (b, g)),
                  pl.BlockSpec((nb, hp * dva, blk), lambda b, g, i: (b, g, 0)),
                  pl.BlockSpec((hp, 1, qw), lambda b, g, i: (g, 0, 0))],
        out_specs=pl.BlockSpec((qw, hp * dh), lambda b, g, i: (b * ns + i, g)),
        out_shape=jax.ShapeDtypeStruct((t, h * dh), BF16),
        scratch_shapes=[pltpu.VMEM((hp, seq, 2 * dh), BF16),
                        pltpu.VMEM((hp, 2 * dh, qw), BF16),
                        pltpu.VMEM((hp, nb, dh), BF16),
                        pltpu.VMEM((hp, nb, dh), BF16),
                        pltpu.VMEM((hp, blk, qw), F32),
                        pltpu.VMEM((hp, blk, qw), F32),
                        pltpu.VMEM((hp, dva, qw), F32)],
        compiler_params=_cparams("parallel", "parallel", "arbitrary"),
        name="moba_attn",
    )(q_t, k, v_t, slopes)


def _merge_kernel(x_ref, h_ref, oa_ref, ob_ref, wg_ref, wa_ref, wb_ref, wo_ref, o_ref):
    d = x_ref.shape[1]
    gates = _sigmoid(jnp.dot(h_ref[...], wg_ref[...], preferred_element_type=F32))
    ya = jnp.dot(oa_ref[...], wa_ref[...], preferred_element_type=F32)
    yb = jnp.dot(ob_ref[...], wb_ref[...], preferred_element_type=F32)
    y = gates[:, :d] * ya + gates[:, d:] * yb
    o_ref[...] = x_ref[...] + jnp.dot(y.astype(BF16), wo_ref[...], preferred_element_type=F32)


def _merge(x, h, oa, ob, wg, wa, wb, wo, li, tm=512):
    t, d = x.shape
    row = lambda i: (i, 0)
    return pl.pallas_call(
        _merge_kernel,
        grid=(t // tm,),
        in_specs=[pl.BlockSpec((tm, d), row), pl.BlockSpec((tm, d), row),
                  pl.BlockSpec((tm, d), row), pl.BlockSpec((tm, d), row),
                  _layer_weight(wg, li), _layer_weight(wa, li),
                  _layer_weight(wb, li), _layer_weight(wo, li)],
        out_specs=pl.BlockSpec((tm, d), row),
        out_shape=jax.ShapeDtypeStruct((t, d), F32),
        compiler_params=_cparams("parallel"),
        name="merge_out_proj",
    )(x, h, oa, ob, wg, wa, wb, wo)


def _mlp_ple_kernel(x_ref, p_ref, gm_ref, wu_ref, wd_ref, gp_ref, wg_ref, wp_ref, gn_ref,
                    o_ref, *maybe_h_ref, nchunk):
    x = x_ref[...]
    h2 = _rms(x, gm_ref[...]).astype(BF16)
    tf = wu_ref.shape[1] // nchunk
    acc = x
    for c in range(nchunk):
        up = jnp.dot(h2, wu_ref[:, c * tf:(c + 1) * tf], preferred_element_type=F32)
        act = jnp.square(jnp.maximum(up, 0.0)).astype(BF16)
        acc = acc + jnp.dot(act, wd_ref[c * tf:(c + 1) * tf, :], preferred_element_type=F32)
    hn = _rms(acc, gp_ref[...]).astype(BF16)
    gate = _sigmoid(jnp.dot(hn, wg_ref[...], preferred_element_type=F32))
    e = jnp.dot(p_ref[...].astype(BF16), wp_ref[...], preferred_element_type=F32)
    xo = acc + gate * e
    o_ref[...] = xo
    if maybe_h_ref:
        maybe_h_ref[0][...] = _rms(xo, gn_ref[...]).astype(BF16)


def _mlp_ple(x, p, g_mlp, wu, wd, g_ple, wg, wp, g_next, li, tm=512, nchunk=4):
    t, d = x.shape
    pd = p.shape[2]
    row = lambda i: (i, 0)
    full = lambda i: (0, 0)
    resident = lambda shape: pl.BlockSpec(shape, full, pipeline_mode=pl.Buffered(1))
    emit_next = g_next is not None
    out_shape = [jax.ShapeDtypeStruct((t, d), F32)]
    out_specs = [pl.BlockSpec((tm, d), row)]
    if emit_next:
        out_shape.append(jax.ShapeDtypeStruct((t, d), BF16))
        out_specs.append(pl.BlockSpec((tm, d), row))
    gn = (g_next if emit_next else g_ple).reshape(1, d)
    res = pl.pallas_call(
        functools.partial(_mlp_ple_kernel, nchunk=nchunk),
        grid=(t // tm,),
        in_specs=[pl.BlockSpec((tm, d), row), pl.BlockSpec((None, tm, pd), lambda i: (li, i, 0)),
                  resident((1, d)), _layer_weight(wu, li), _layer_weight(wd, li),
                  resident((1, d)), _layer_weight(wg, li), _layer_weight(wp, li),
                  resident((1, d))],
        out_specs=out_specs,
        out_shape=out_shape,
        compiler_params=_cparams("parallel"),
        name="mlp_ple",
    )(x, p, g_mlp.reshape(1, d), wu, wd, g_ple.reshape(1, d), wg, wp, gn)
    return (res[0], res[1]) if emit_next else (res[0], None)


def kernel(x, p, norm_mix, w_in, gla_gate_w2, gla_gate_b, gla_out_norm, moba_q_norm,
           moba_k_norm, w_branch_a, w_branch_b, w_out, norm_mlp, w_up, w_down,
           norm_ple, w_ple_gate, w_ple):
    batch, seq, d = x.shape
    depth = w_in.shape[0]
    t = batch * seq
    x = x.reshape(t, d)

    slopes = 2.0 ** (-8.0 * jnp.arange(1, MOBA_HEADS + 1, dtype=F32) / MOBA_HEADS)
    slopes = jnp.broadcast_to(slopes[:, None, None],
                              (MOBA_HEADS, 1, MOBA_QBLOCKS_PER_STEP * MOBA_BLOCK))

    w_gqk, w_gv, w_lr, w_gr, w_mq_t, w_mk, w_mv_t, w_gate = _split_w_in(w_in)
    w2 = jnp.pad(gla_gate_w2, ((0, 0), (0, LANES - GLA_GATE_RANK), (0, 0))).astype(BF16)
    w_a, w_b, w_o = (w.astype(BF16) for w in (w_branch_a, w_branch_b, w_out))
    w_u, w_d, w_pg, w_pe = (w.astype(BF16) for w in (w_up, w_down, w_ple_gate, w_ple))
    p = p.reshape(depth, t, -1)

    h = _norm_cast(x, norm_mix[0])
    for li in range(depth):
        mq_t, mk, mv_t, la_hi, la_lo = _moba_proj(
            h, w_mq_t, w_mk, w_mv_t, moba_q_norm[li], moba_k_norm[li], w_lr, w2,
            gla_gate_b[li].reshape(1, -1), li)

        oa = _gla(h, la_hi, la_lo, w_gqk, w_gv, w_gr, gla_out_norm[li], li, batch, seq)
        ob = _moba(mq_t, mk, mv_t, slopes, batch, seq)

        x = _merge(x, h, oa, ob, w_gate, w_a, w_b, w_o, li)
        g_next = norm_mix[li + 1] if li + 1 < depth else None
        x, h = _mlp_ple(x, p, norm_mlp[li], w_u, w_d, norm_ple[li], w_pg, w_pe, g_next, li)
    return x.reshape(batch, seq, d)
```

```python
import functools

import jax
import jax.numpy as jnp
from jax import lax
from jax.experimental import pallas as pl
from jax.experimental.pallas import tpu as pltpu

F32 = jnp.float32
BF16 = jnp.bfloat16

EPS = 1e-6
GLA_HEADS = 4
GLA_DK = 128
GLA_DV = 256
GLA_GATE_RANK = 16
GLA_GATE_TAU = 16.0
GLA_CHUNK = 64
GLA_ROW_BLOCK = 256
MOBA_HEADS = 8
MOBA_DH = 128
MOBA_BLOCK = 256
MOBA_TOPK = 3
MOBA_HEADS_PER_STEP = 4
MOBA_ONES_ROWS = 16
MOBA_DV_AUG = MOBA_DH + MOBA_ONES_ROWS
LOG2E = 1.4426950408889634

LANES = 128
SUBLANES = 8
VMEM_LIMIT = 48 * 1024 * 1024
NEG_BIG = -1e30

NT_DIMS = (((1,), (1,)), ((), ()))
TN_DIMS = (((0,), (0,)), ((), ()))


def _cparams(*sem):
    return pltpu.CompilerParams(dimension_semantics=sem, vmem_limit_bytes=VMEM_LIMIT)


def _layer_weight(stacked, li):
    return pl.BlockSpec((None,) + stacked.shape[1:], lambda *_: (li, 0, 0),
                        pipeline_mode=pl.Buffered(1))


def _rms(x, g):
    return x * lax.rsqrt(jnp.mean(x * x, axis=-1, keepdims=True) + EPS) * g


def _sigmoid(x):
    return 1.0 / (1.0 + jnp.exp(-x))


def _norm_kernel(x_ref, g_ref, o_ref):
    o_ref[...] = _rms(x_ref[...], g_ref[...]).astype(o_ref.dtype)


def _norm_cast(x, g, tm=1024):
    t, d = x.shape
    return pl.pallas_call(
        _norm_kernel,
        grid=(t // tm,),
        in_specs=[pl.BlockSpec((tm, d), lambda i: (i, 0)),
                  pl.BlockSpec((1, d), lambda i: (0, 0))],
        out_specs=pl.BlockSpec((tm, d), lambda i: (i, 0)),
        out_shape=jax.ShapeDtypeStruct((t, d), BF16),
        compiler_params=_cparams("parallel"),
        name="norm_cast",
    )(x, g.reshape(1, d))


def _split_w_in_kernel(wt_ref, gqk_ref, gv_ref, lr_ref, gr_ref, mqt_ref, mk_ref, mvt_ref, gate_ref):
    offs = [0]
    for ref in (gqk_ref, gv_ref):
        offs.append(offs[-1] + ref.shape[1])
    offs.append(offs[-1] + GLA_GATE_RANK)
    for n in (gr_ref.shape[1], mqt_ref.shape[0], mk_ref.shape[1], mvt_ref.shape[0],
              gate_ref.shape[1]):
        offs.append(offs[-1] + n)
    piece_t = lambda i: wt_ref[offs[i]:offs[i + 1], :]
    gqk_ref[...] = piece_t(0).T.astype(BF16)
    gv_ref[...] = piece_t(1).T.astype(BF16)
    lr_t = wt_ref[offs[2]:offs[2] + LANES, :].T
    lane = lax.broadcasted_iota(jnp.int32, lr_t.shape, 1)
    lr_ref[...] = jnp.where(lane < GLA_GATE_RANK, lr_t, 0.0).astype(BF16)
    gr_ref[...] = piece_t(3).T.astype(BF16)
    mqt_ref[...] = piece_t(4).astype(BF16)
    mk_ref[...] = piece_t(5).T.astype(BF16)
    mvt_ref[...] = piece_t(6).astype(BF16)
    gate_ref[...] = piece_t(7).T.astype(BF16)


def _split_w_in(w_in, tc=256):
    depth, d, n_in = w_in.shape
    n_gqk = 2 * GLA_HEADS * GLA_DK
    n_gv = GLA_HEADS * GLA_DV
    n_m = MOBA_HEADS * MOBA_DH
    assert n_in == n_gqk + n_gv + GLA_GATE_RANK + n_gv + 3 * n_m + 2 * d
    rows = lambda l, i: (l, i, 0)
    cols = lambda l, i: (l, 0, i)
    row_major = lambda n: ((depth, d, n), pl.BlockSpec((None, tc, n), rows))
    transposed = lambda n: ((depth, n, d), pl.BlockSpec((None, n, tc), cols))
    outs = [row_major(n_gqk), row_major(n_gv), row_major(LANES), row_major(n_gv),
            transposed(n_m), row_major(n_m), transposed(n_m), row_major(2 * d)]
    return pl.pallas_call(
        _split_w_in_kernel,
        grid=(depth, d // tc),
        in_specs=[pl.BlockSpec((None, n_in, tc), cols)],
        out_specs=[spec for _, spec in outs],
        out_shape=[jax.ShapeDtypeStruct(shape, BF16) for shape, _ in outs],
        compiler_params=_cparams("parallel", "parallel"),
        name="split_w_in",
    )(jnp.swapaxes(w_in, 1, 2))


def _moba_proj_kernel(h_ref, wqt_ref, wk_ref, wvt_ref, gq_ref, gk_ref, wlr_ref, w2_ref, b_ref,
                      qt_ref, k_ref, vt_ref, lah_ref, lal_ref, *, scale):
    dh = MOBA_DH
    blk = qt_ref.shape[2]
    h = h_ref[...]
    nheads = k_ref.shape[1] // dh

    lr = jnp.dot(h, wlr_ref[...], preferred_element_type=F32)
    q_t = lax.dot_general(wqt_ref[...], h, NT_DIMS, preferred_element_type=F32)
    gq = gq_ref[...] * scale
    segs = []
    for hh in range(nheads):
        seg = q_t[hh * dh:(hh + 1) * dh, :]
        ms = jnp.mean(seg * seg, axis=0, keepdims=True)
        segs.append((seg * lax.rsqrt(ms + EPS) * gq).astype(qt_ref.dtype))
    q_t = jnp.concatenate(segs, axis=0)

    z = jnp.dot(lr.astype(BF16), w2_ref[...], preferred_element_type=F32) + b_ref[...]
    k = jnp.dot(h, wk_ref[...], preferred_element_type=F32)
    gk = gk_ref[...]
    for hh in range(nheads):
        k_ref[:, hh * dh:(hh + 1) * dh] = _rms(k[:, hh * dh:(hh + 1) * dh], gk).astype(k_ref.dtype)

    v_t = lax.dot_general(wvt_ref[...], h, NT_DIMS, preferred_element_type=F32)
    ones = jnp.ones((MOBA_ONES_ROWS, v_t.shape[1]), vt_ref.dtype)
    parts = []
    for hh in range(nheads):
        parts += [v_t[hh * dh:(hh + 1) * dh, :].astype(vt_ref.dtype), ones]
    v_t = jnp.concatenate(parts, axis=0)
    for c in range(qt_ref.shape[0]):
        qt_ref[c] = q_t[:, c * blk:(c + 1) * blk]
        vt_ref[c] = v_t[:, c * blk:(c + 1) * blk]

    log_a = (jnp.minimum(z, 0.0) - jnp.log(1.0 + jnp.exp(-jnp.abs(z)))) * (1.0 / GLA_GATE_TAU)
    la_hi = log_a.astype(BF16)
    lah_ref[...] = la_hi
    lal_ref[...] = (log_a - la_hi.astype(F32)).astype(BF16)


def _moba_proj(h, wq_t, wk, wv_t, gq, gk, w_lr, w2, b, li, tm=1024, blk=MOBA_BLOCK):
    t, d = h.shape
    n = wk.shape[2]
    n_la = w2.shape[2]
    full = lambda i: (0, 0)
    resident = lambda shape: pl.BlockSpec(shape, full, pipeline_mode=pl.Buffered(1))
    nv = n // MOBA_DH * MOBA_DV_AUG
    q_spec = pl.BlockSpec((tm // blk, n, blk), lambda i: (i, 0, 0))
    v_spec = pl.BlockSpec((tm // blk, nv, blk), lambda i: (i, 0, 0))
    q_shape = jax.ShapeDtypeStruct((t // blk, n, blk), BF16)
    v_shape = jax.ShapeDtypeStruct((t // blk, nv, blk), BF16)
    return pl.pallas_call(
        functools.partial(_moba_proj_kernel, scale=MOBA_DH ** -0.5 * LOG2E),
        grid=(t // tm,),
        in_specs=[pl.BlockSpec((tm, d), lambda i: (i, 0)),
                  _layer_weight(wq_t, li), _layer_weight(wk, li), _layer_weight(wv_t, li),
                  resident((MOBA_DH, 1)), resident((1, MOBA_DH)),
                  _layer_weight(w_lr, li), _layer_weight(w2, li), resident((1, n_la))],
        out_specs=[q_spec, pl.BlockSpec((tm, n), lambda i: (i, 0)), v_spec,
                   pl.BlockSpec((tm, n_la), lambda i: (i, 0)),
                   pl.BlockSpec((tm, n_la), lambda i: (i, 0))],
        out_shape=[q_shape, jax.ShapeDtypeStruct((t, n), BF16), v_shape,
                   jax.ShapeDtypeStruct((t, n_la), BF16), jax.ShapeDtypeStruct((t, n_la), BF16)],
        compiler_params=_cparams("parallel"),
        name="moba_qkv_proj",
    )(h, wq_t, wk, wv_t, gq.reshape(MOBA_DH, 1), gk.reshape(1, MOBA_DH), w_lr, w2, b)


def _gla_kernel(h_ref, lah_ref, lal_ref, wqk_ref, wv_ref, wr_ref, gn_ref, o_ref,
                st_ref, tri_ref, *, rb):
    n_k = GLA_HEADS * GLA_DK
    h = h_ref[...]
    qk_all = jnp.dot(h, wqk_ref[...], preferred_element_type=F32)

    c = GLA_CHUNK
    nc = rb // c
    shift = c.bit_length() - 1
    row = lax.broadcasted_iota(jnp.int32, (rb, rb), 0)
    col = lax.broadcasted_iota(jnp.int32, (rb, rb), 1)
    same_chunk = (row >> shift) == (col >> shift)
    causal = same_chunk & (row >= col)

    @pl.when(pl.program_id(1) == 0)
    def _():
        st_ref[...] = jnp.zeros_like(st_ref)
        tri_ref[...] = causal.astype(BF16)

    tri = tri_ref[...]
    heads = range(GLA_HEADS)
    kcs = [slice(hh * GLA_DK, (hh + 1) * GLA_DK) for hh in heads]
    vcs = [slice(hh * GLA_DV, (hh + 1) * GLA_DV) for hh in heads]
    b = (jnp.dot(tri, lah_ref[...], preferred_element_type=F32)
         + jnp.dot(tri, lal_ref[...], preferred_element_type=F32))
    v_all = jnp.dot(h, wv_ref[...], preferred_element_type=F32).astype(BF16)
    g_r = jnp.dot(h, wr_ref[...], preferred_element_type=F32)
    gr_all = g_r * _sigmoid(g_r)
    b_last = jnp.concatenate(
        [jnp.broadcast_to(b[ci * c + c - 1:ci * c + c, :], (c, b.shape[1])) for ci in range(nc)],
        axis=0)
    q_all = qk_all[:, :n_k]
    k_all = qk_all[:, n_k:]
    decay = jnp.exp(b)
    qd = (q_all * decay * (GLA_DK ** -0.5)).astype(BF16)
    kd = (k_all * jnp.exp(-b)).astype(BF16)
    kl = (k_all * jnp.exp(b_last - b)).astype(BF16)
    a = [lax.dot_general(qd[:, kcs[hh]], kd[:, kcs[hh]], NT_DIMS, preferred_element_type=F32)
         for hh in heads]
    chunks = [slice(ci * c, (ci + 1) * c) for ci in range(nc)]
    kv_t = [[lax.dot_general(v_all[rs, vcs[hh]], kl[rs, kcs[hh]], TN_DIMS,
                             preferred_element_type=F32) for hh in heads] for rs in chunks]
    o_intra = [jnp.dot(jnp.where(causal, a[hh], 0.0).astype(BF16), v_all[:, vcs[hh]],
                       preferred_element_type=F32) for hh in heads]
    st = [st_ref[hh] for hh in heads]
    o_inter = []
    for ci, rs in enumerate(chunks):
        o_inter.append([lax.dot_general(qd[rs, kcs[hh]], st[hh].astype(BF16), NT_DIMS,
                                        preferred_element_type=F32) for hh in heads])
        last = ci * c + c - 1
        st = [st[hh] * decay[last:last + 1, kcs[hh]] + kv_t[ci][hh] for hh in heads]
    for hh in heads:
        st_ref[hh] = st[hh]
    for ci, rs in enumerate(chunks):
        for hh in heads:
            o = o_intra[hh][rs] + o_inter[ci][hh]
            o_ref[rs, vcs[hh]] = (_rms(o, gn_ref[hh]) * gr_all[rs, vcs[hh]]).astype(o_ref.dtype)


def _gla(h, la_hi, la_lo, w_qk, w_v, w_r, gnorm, li, batch, seq, rb=GLA_ROW_BLOCK):
    t, d = h.shape
    nblk = seq // rb
    nh = GLA_HEADS
    rows = lambda bi, s: (bi * nblk + s, 0)
    return pl.pallas_call(
        functools.partial(_gla_kernel, rb=rb),
        grid=(batch, nblk),
        in_specs=[pl.BlockSpec((rb, d), rows), pl.BlockSpec((rb, la_hi.shape[1]), rows),
                  pl.BlockSpec((rb, la_lo.shape[1]), rows),
                  _layer_weight(w_qk, li), _layer_weight(w_v, li), _layer_weight(w_r, li),
                  pl.BlockSpec((nh, 1, GLA_DV), lambda bi, s: (0, 0, 0))],
        out_specs=pl.BlockSpec((rb, nh * GLA_DV), lambda bi, s: (bi * nblk + s, 0)),
        out_shape=jax.ShapeDtypeStruct((t, nh * GLA_DV), BF16),
        scratch_shapes=[pltpu.VMEM((nh, GLA_DV, GLA_DK), F32),
                        pltpu.VMEM((rb, rb), BF16)],
        compiler_params=_cparams("parallel", "arbitrary"),
        name="gla_branch",
    )(h, la_hi, la_lo, w_qk, w_v, w_r, gnorm.reshape(nh, 1, GLA_DV))


def _alibi_split(slope):
    c = slope * LOG2E
    c_hi = c.astype(BF16).astype(F32)
    return c_hi, c - c_hi


def _moba_build_keys(k, slope, kaug_ref, kmh_ref, kml_ref, nb):
    blk, dh = MOBA_BLOCK, MOBA_DH
    col = lax.broadcasted_iota(jnp.int32, (blk, dh), 1)
    r_key = lax.broadcasted_iota(jnp.int32, (blk, dh), 0).astype(F32)
    c_hi, c_lo = _alibi_split(slope[:, :dh])
    base = jnp.where((col == nb) | (col == nb + 4), c_hi, 0.0)
    base = jnp.where((col == nb + 1) | (col == nb + 5), c_lo, base)
    base = jnp.where((col == nb + 2) | (col == nb + 3), r_key, base)
    is_offset = (col == nb + 6) | (col == nb + 7)
    ones = jnp.ones((SUBLANES, blk), BF16)
    sums = []
    for n in range(nb):
        kn = k[n * blk:(n + 1) * blk, :]
        kaug_ref[n * blk:(n + 1) * blk, :dh] = kn
        e = jnp.where(col == n, 1.0, jnp.where(is_offset, float(n * blk), base))
        kaug_ref[n * blk:(n + 1) * blk, dh:] = e.astype(BF16)
        sums.append(jnp.dot(ones, kn, preferred_element_type=F32)[:1])
    km = jnp.concatenate(sums, axis=0) * (1.0 / blk)
    km_hi = km.astype(BF16)
    kmh_ref[...] = km_hi
    kml_ref[...] = (km - km_hi.astype(F32)).astype(BF16)


def _moba_aug_queries(q_t, g_t, slope, i, nb):
    blk, dh = MOBA_BLOCK, MOBA_DH
    nidx = lax.broadcasted_iota(jnp.int32, (nb, blk), 0)
    valid = nidx < i
    g = jnp.where(valid, g_t, -jnp.inf)
    rank = jnp.zeros((nb, blk), jnp.int32)
    for m in range(nb):
        gm = g[m:m + 1, :]
        beats = (gm > g) | ((gm == g) & (nidx > m))
        rank = rank + beats.astype(jnp.int32)
    selb = jnp.where(valid & (rank < MOBA_TOPK), 0.0, NEG_BIG)

    ridx = lax.broadcasted_iota(jnp.int32, (dh - nb, blk), 0) + nb
    r_t = lax.broadcasted_iota(jnp.int32, (dh - nb, blk), 1).astype(F32)
    off = (i * blk).astype(F32)
    c_hi, c_lo = _alibi_split(slope)
    rest = jnp.where((ridx == nb) | (ridx == nb + 1), -r_t, 0.0)
    rest = jnp.where((ridx == nb + 2) | (ridx == nb + 6), c_hi, rest)
    rest = jnp.where((ridx == nb + 3) | (ridx == nb + 7), c_lo, rest)
    rest = jnp.where((ridx == nb + 4) | (ridx == nb + 5), -off, rest)
    x_t = jnp.concatenate([selb, rest], axis=0)
    return jnp.concatenate([q_t, x_t.astype(BF16)], axis=0)


def _moba_kernel(qt_ref, k_ref, vt_ref, slope_ref, o_ref, kaug_ref, qaug_ref, kmh_ref, kml_ref,
                 s0_ref, s1_ref, acc_ref, *, nb, hp):
    blk, dh, dva = MOBA_BLOCK, MOBA_DH, MOBA_DV_AUG
    i = pl.program_id(2)
    hs = [slice(hh * dh, (hh + 1) * dh) for hh in range(hp)]
    vs = [slice(hh * dva, (hh + 1) * dva) for hh in range(hp)]

    @pl.when(i == 0)
    def _():
        for hh in range(hp):
            _moba_build_keys(k_ref[:, hs[hh]], slope_ref[hh], kaug_ref.at[hh],
                             kmh_ref.at[hh], kml_ref.at[hh], nb)

    own_rows = pl.ds(pl.multiple_of(i * blk, blk), blk)
    owns = [jnp.dot(k_ref[own_rows, hs[hh]], qt_ref[hs[hh], :], preferred_element_type=F32)
            for hh in range(hp)]
    gates = [jnp.dot(kmh_ref[hh], qt_ref[hs[hh], :], preferred_element_type=F32)
             + jnp.dot(kml_ref[hh], qt_ref[hs[hh], :], preferred_element_type=F32)
             for hh in range(hp)]
    for hh in range(hp):
        qaug_ref[hh] = _moba_aug_queries(qt_ref[hs[hh], :], gates[hh], slope_ref[hh], i, nb)

    def past_scores(hh, j):
        rows = pl.ds(pl.multiple_of(j * blk, blk), blk)
        s = jnp.dot(kaug_ref[hh, rows, :], qaug_ref[hh], preferred_element_type=F32)
        return s, jnp.max(s, axis=0, keepdims=True)

    def update(hh, s, s_max, j, m):
        m_new = jnp.maximum(m, s_max)
        alpha = jnp.exp2(m - m_new)
        p = jnp.exp2((s - m_new).astype(BF16))
        pv = jnp.dot(vt_ref[j, vs[hh], :], p, preferred_element_type=F32)
        acc_ref[hh] = alpha * acc_ref[hh] + pv
        return m_new

    kk = lax.broadcasted_iota(jnp.int32, (blk, blk), 0)
    qq = lax.broadcasted_iota(jnp.int32, (blk, blk), 1)
    dist = (qq - kk).astype(F32)
    ms = []
    for hh in range(hp):
        acc_ref[hh] = jnp.zeros((dva, blk), F32)
        own = jnp.where(qq >= kk, owns[hh] - (slope_ref[hh] * LOG2E) * dist, NEG_BIG)
        ms.append(update(hh, own, jnp.max(own, axis=0, keepdims=True), i,
                         jnp.full((1, blk), -jnp.inf, F32)))
    carry = []
    for hh in range(hp):
        s, s_max = past_scores(hh, 0)
        s0_ref[hh] = s
        carry.append((ms[hh], s_max))

    def body(jj, carry):
        j0 = 2 * jj
        j2 = jnp.minimum(j0 + 2, nb - 1)
        ms = [c[0] for c in carry]
        max0 = [c[1] for c in carry]
        max1 = []
        for hh in range(hp):
            s, s_max = past_scores(hh, j0 + 1)
            s1_ref[hh] = s
            max1.append(s_max)
        for hh in range(hp):
            ms[hh] = update(hh, s0_ref[hh], max0[hh], j0, ms[hh])
        for hh in range(hp):
            s, max0[hh] = past_scores(hh, j2)
            s0_ref[hh] = s
        for hh in range(hp):
            ms[hh] = update(hh, s1_ref[hh], max1[hh], j0 + 1, ms[hh])
        return tuple(zip(ms, max0))

    def last_block(_, carry):
        return tuple((update(hh, s0_ref[hh], carry[hh][1], i - 1, carry[hh][0]),
                      carry[hh][1]) for hh in range(hp))

    def trips(n_pairs):
        def many(kk, c):
            for r in range(n_pairs):
                c = body(n_pairs * kk + r, c)
            return c
        return many

    pairs_done = 0
    carry = tuple(carry)
    for n_pairs in (4, 2, 1):
        n_trips = (i // 2 - pairs_done) // n_pairs
        start = pairs_done // n_pairs
        carry = lax.fori_loop(start, start + n_trips, trips(n_pairs), carry)
        pairs_done = pairs_done + n_trips * n_pairs
    lax.fori_loop(0, i % 2, last_block, carry)
    for hh in range(hp):
        acc = acc_ref[hh]
        o_ref[:, hs[hh]] = (acc[:dh] / acc[dh:dh + 1]).T.astype(o_ref.dtype)


def _moba(q_t, k, v_t, slopes, batch, seq, hp=MOBA_HEADS_PER_STEP):
    t = k.shape[0]
    h, dh, dva, blk = MOBA_HEADS, MOBA_DH, MOBA_DV_AUG, MOBA_BLOCK
    nb = seq // blk
    hg = h // hp
    return pl.pallas_call(
        functools.partial(_moba_kernel, nb=nb, hp=hp),
        grid=(batch, hg, nb),
        in_specs=[pl.BlockSpec((None, hp * dh, blk), lambda b, g, i: (b * nb + i, g, 0)),
                  pl.BlockSpec((seq, hp * dh), lambda b, g, i: (b, g)),
                  pl.BlockSpec((nb, hp * dva, blk), lambda b, g, i: (b, g, 0)),
                  pl.BlockSpec((hp, 1, blk), lambda b, g, i: (g, 0, 0))],
        out_specs=pl.BlockSpec((blk, hp * dh), lambda b, g, i: (b * nb + i, g)),
        out_shape=jax.ShapeDtypeStruct((t, h * dh), BF16),
        scratch_shapes=[pltpu.VMEM((hp, seq, 2 * dh), BF16),
                        pltpu.VMEM((hp, 2 * dh, blk), BF16),
                        pltpu.VMEM((hp, nb, dh), BF16),
                        pltpu.VMEM((hp, nb, dh), BF16),
                        pltpu.VMEM((hp, blk, blk), F32),
                        pltpu.VMEM((hp, blk, blk), F32),
                        pltpu.VMEM((hp, dva, blk), F32)],
        compiler_params=_cparams("parallel", "parallel", "arbitrary"),
        name="moba_attn",
    )(q_t, k, v_t, slopes)


def _merge_kernel(x_ref, h_ref, oa_ref, ob_ref, wg_ref, wa_ref, wb_ref, wo_ref, o_ref):
    d = x_ref.shape[1]
    gates = _sigmoid(jnp.dot(h_ref[...], wg_ref[...], preferred_element_type=F32))
    ya = jnp.dot(oa_ref[...], wa_ref[...], preferred_element_type=F32)
    yb = jnp.dot(ob_ref[...], wb_ref[...], preferred_element_type=F32)
    y = gates[:, :d] * ya + gates[:, d:] * yb
    o_ref[...] = x_ref[...] + jnp.dot(y.astype(BF16), wo_ref[...], preferred_element_type=F32)


def _merge(x, h, oa, ob, wg, wa, wb, wo, li, tm=512):
    t, d = x.shape
    row = lambda i: (i, 0)
    return pl.pallas_call(
        _merge_kernel,
        grid=(t // tm,),
        in_specs=[pl.BlockSpec((tm, d), row), pl.BlockSpec((tm, d), row),
                  pl.BlockSpec((tm, d), row), pl.BlockSpec((tm, d), row),
                  _layer_weight(wg, li), _layer_weight(wa, li),
                  _layer_weight(wb, li), _layer_weight(wo, li)],
        out_specs=pl.BlockSpec((tm, d), row),
        out_shape=jax.ShapeDtypeStruct((t, d), F32),
        compiler_params=_cparams("parallel"),
        name="merge_out_proj",
    )(x, h, oa, ob, wg, wa, wb, wo)


def _mlp_ple_kernel(x_ref, p_ref, gm_ref, wu_ref, wd_ref, gp_ref, wg_ref, wp_ref, gn_ref,
                    o_ref, *maybe_h_ref, nchunk):
    x = x_ref[...]
    h2 = _rms(x, gm_ref[...]).astype(BF16)
    tf = wu_ref.shape[1] // nchunk
    acc = x
    for c in range(nchunk):
        up = jnp.dot(h2, wu_ref[:, c * tf:(c + 1) * tf], preferred_element_type=F32)
        act = jnp.square(jnp.maximum(up, 0.0)).astype(BF16)
        acc = acc + jnp.dot(act, wd_ref[c * tf:(c + 1) * tf, :], preferred_element_type=F32)
    hn = _rms(acc, gp_ref[...]).astype(BF16)
    gate = _sigmoid(jnp.dot(hn, wg_ref[...], preferred_element_type=F32))
    e = jnp.dot(p_ref[...].astype(BF16), wp_ref[...], preferred_element_type=F32)
    xo = acc + gate * e
    o_ref[...] = xo
    if maybe_h_ref:
        maybe_h_ref[0][...] = _rms(xo, gn_ref[...]).astype(BF16)


def _mlp_ple(x, p, g_mlp, wu, wd, g_ple, wg, wp, g_next, li, tm=512, nchunk=4):
    t, d = x.shape
    pd = p.shape[2]
    row = lambda i: (i, 0)
    full = lambda i: (0, 0)
    resident = lambda shape: pl.BlockSpec(shape, full, pipeline_mode=pl.Buffered(1))
    emit_next = g_next is not None
    out_shape = [jax.ShapeDtypeStruct((t, d), F32)]
    out_specs = [pl.BlockSpec((tm, d), row)]
    if emit_next:
        out_shape.append(jax.ShapeDtypeStruct((t, d), BF16))
        out_specs.append(pl.BlockSpec((tm, d), row))
    gn = (g_next if emit_next else g_ple).reshape(1, d)
    res = pl.pallas_call(
        functools.partial(_mlp_ple_kernel, nchunk=nchunk),
        grid=(t // tm,),
        in_specs=[pl.BlockSpec((tm, d), row), pl.BlockSpec((None, tm, pd), lambda i: (li, i, 0)),
                  resident((1, d)), _layer_weight(wu, li), _layer_weight(wd, li),
                  resident((1, d)), _layer_weight(wg, li), _layer_weight(wp, li),
                  resident((1, d))],
        out_specs=out_specs,
        out_shape=out_shape,
        compiler_params=_cparams("parallel"),
        name="mlp_ple",
    )(x, p, g_mlp.reshape(1, d), wu, wd, g_ple.reshape(1, d), wg, wp, gn)
    return (res[0], res[1]) if emit_next else (res[0], None)


def kernel(x, p, norm_mix, w_in, gla_gate_w2, gla_gate_b, gla_out_norm, moba_q_norm,
           moba_k_norm, w_branch_a, w_branch_b, w_out, norm_mlp, w_up, w_down,
           norm_ple, w_ple_gate, w_ple):
    batch, seq, d = x.shape
    depth = w_in.shape[0]
    t = batch * seq
    x = x.reshape(t, d)

    slopes = 2.0 ** (-8.0 * jnp.arange(1, MOBA_HEADS + 1, dtype=F32) / MOBA_HEADS)
    slopes = jnp.broadcast_to(slopes[:, None, None], (MOBA_HEADS, 1, MOBA_BLOCK))

    w_gqk, w_gv, w_lr, w_gr, w_mq_t, w_mk, w_mv_t, w_gate = _split_w_in(w_in)
    w2 = jnp.pad(gla_gate_w2, ((0, 0), (0, LANES - GLA_GATE_RANK), (0, 0))).astype(BF16)
    w_a, w_b, w_o = (w.astype(BF16) for w in (w_branch_a, w_branch_b, w_out))
    w_u, w_d, w_pg, w_pe = (w.astype(BF16) for w in (w_up, w_down, w_ple_gate, w_ple))
    p = p.reshape(depth, t, -1)

    h = _norm_cast(x, norm_mix[0])
    for li in range(depth):
        mq_t, mk, mv_t, la_hi, la_lo = _moba_proj(
            h, w_mq_t, w_mk, w_mv_t, moba_q_norm[li], moba_k_norm[li], w_lr, w2,
            gla_gate_b[li].reshape(1, -1), li)

        oa = _gla(h, la_hi, la_lo, w_gqk, w_gv, w_gr, gla_out_norm[li], li, batch, seq)
        ob = _moba(mq_t, mk, mv_t, slopes, batch, seq)

        x = _merge(x, h, oa, ob, w_gate, w_a, w_b, w_o, li)
        g_next = norm_mix[li + 1] if li + 1 < depth else None
        x, h = _mlp_ple(x, p, norm_mlp[li], w_u, w_d, norm_ple[li], w_pg, w_pe, g_next, li)
    return x.reshape(batch, seq, d)
```

```python
import functools

import jax
import jax.numpy as jnp
from jax import lax
from jax.experimental import pallas as pl
from jax.experimental.pallas import tpu as pltpu

F32 = jnp.float32
BF16 = jnp.bfloat16

EPS = 1e-6
GLA_HEADS = 4
GLA_DK = 128
GLA_DV = 256
GLA_GATE_RANK = 16
GLA_GATE_TAU = 16.0
GLA_CHUNK = 64
GLA_ROW_BLOCK = 256
MOBA_HEADS = 8
MOBA_DH = 128
MOBA_BLOCK = 256
MOBA_TOPK = 3
MOBA_HEADS_PER_STEP = 4
MOBA_ONES_ROWS = 16
MOBA_DV_AUG = MOBA_DH + MOBA_ONES_ROWS
LOG2E = 1.4426950408889634

LANES = 128
SUBLANES = 8
VMEM_LIMIT = 48 * 1024 * 1024
NEG_BIG = -1e30

NT_DIMS = (((1,), (1,)), ((), ()))
TN_DIMS = (((0,), (0,)), ((), ()))


def _cparams(*sem):
    return pltpu.CompilerParams(dimension_semantics=sem, vmem_limit_bytes=VMEM_LIMIT)


def _layer_weight(stacked, li):
    return pl.BlockSpec((None,) + stacked.shape[1:], lambda *_: (li, 0, 0),
                        pipeline_mode=pl.Buffered(1))


def _rms(x, g):
    return x * lax.rsqrt(jnp.mean(x * x, axis=-1, keepdims=True) + EPS) * g


def _sigmoid(x):
    return 1.0 / (1.0 + jnp.exp(-x))


def _norm_kernel(x_ref, g_ref, o_ref):
    o_ref[...] = _rms(x_ref[...], g_ref[...]).astype(o_ref.dtype)


def _norm_cast(x, g, tm=1024):
    t, d = x.shape
    return pl.pallas_call(
        _norm_kernel,
        grid=(t // tm,),
        in_specs=[pl.BlockSpec((tm, d), lambda i: (i, 0)),
                  pl.BlockSpec((1, d), lambda i: (0, 0))],
        out_specs=pl.BlockSpec((tm, d), lambda i: (i, 0)),
        out_shape=jax.ShapeDtypeStruct((t, d), BF16),
        compiler_params=_cparams("parallel"),
        name="norm_cast",
    )(x, g.reshape(1, d))


def _split_w_in_kernel(wt_ref, gqk_ref, gv_ref, lr_ref, gr_ref, mqt_ref, mk_ref, mvt_ref, gate_ref):
    offs = [0]
    for ref in (gqk_ref, gv_ref):
        offs.append(offs[-1] + ref.shape[1])
    offs.append(offs[-1] + GLA_GATE_RANK)
    for n in (gr_ref.shape[1], mqt_ref.shape[0], mk_ref.shape[1], mvt_ref.shape[0],
              gate_ref.shape[1]):
        offs.append(offs[-1] + n)
    piece_t = lambda i: wt_ref[offs[i]:offs[i + 1], :]
    gqk_ref[...] = piece_t(0).T.astype(BF16)
    gv_ref[...] = piece_t(1).T.astype(BF16)
    lr_t = wt_ref[offs[2]:offs[2] + LANES, :].T
    lane = lax.broadcasted_iota(jnp.int32, lr_t.shape, 1)
    lr_ref[...] = jnp.where(lane < GLA_GATE_RANK, lr_t, 0.0).astype(BF16)
    gr_ref[...] = piece_t(3).T.astype(BF16)
    mqt_ref[...] = piece_t(4).astype(BF16)
    mk_ref[...] = piece_t(5).T.astype(BF16)
    mvt_ref[...] = piece_t(6).astype(BF16)
    gate_ref[...] = piece_t(7).T.astype(BF16)


def _split_w_in(w_in, tc=256):
    depth, d, n_in = w_in.shape
    n_gqk = 2 * GLA_HEADS * GLA_DK
    n_gv = GLA_HEADS * GLA_DV
    n_m = MOBA_HEADS * MOBA_DH
    assert n_in == n_gqk + n_gv + GLA_GATE_RANK + n_gv + 3 * n_m + 2 * d
    rows = lambda l, i: (l, i, 0)
    cols = lambda l, i: (l, 0, i)
    row_major = lambda n: ((depth, d, n), pl.BlockSpec((None, tc, n), rows))
    transposed = lambda n: ((depth, n, d), pl.BlockSpec((None, n, tc), cols))
    outs = [row_major(n_gqk), row_major(n_gv), row_major(LANES), row_major(n_gv),
            transposed(n_m), row_major(n_m), transposed(n_m), row_major(2 * d)]
    return pl.pallas_call(
        _split_w_in_kernel,
        grid=(depth, d // tc),
        in_specs=[pl.BlockSpec((None, n_in, tc), cols)],
        out_specs=[spec for _, spec in outs],
        out_shape=[jax.ShapeDtypeStruct(shape, BF16) for shape, _ in outs],
        compiler_params=_cparams("parallel", "parallel"),
        name="split_w_in",
    )(jnp.swapaxes(w_in, 1, 2))


def _moba_proj_kernel(h_ref, wqt_ref, wk_ref, wvt_ref, gq_ref, gk_ref, wlr_ref, w2_ref, b_ref,
                      qt_ref, k_ref, vt_ref, lah_ref, lal_ref, *, scale):
    dh = MOBA_DH
    blk = qt_ref.shape[2]
    h = h_ref[...]
    nheads = k_ref.shape[1] // dh

    lr = jnp.dot(h, wlr_ref[...], preferred_element_type=F32)
    q_t = lax.dot_general(wqt_ref[...], h, NT_DIMS, preferred_element_type=F32)
    gq = gq_ref[...] * scale
    segs = []
    for hh in range(nheads):
        seg = q_t[hh * dh:(hh + 1) * dh, :]
        ms = jnp.mean(seg * seg, axis=0, keepdims=True)
        segs.append((seg * lax.rsqrt(ms + EPS) * gq).astype(qt_ref.dtype))
    q_t = jnp.concatenate(segs, axis=0)

    z = jnp.dot(lr.astype(BF16), w2_ref[...], preferred_element_type=F32) + b_ref[...]
    k = jnp.dot(h, wk_ref[...], preferred_element_type=F32)
    gk = gk_ref[...]
    for hh in range(nheads):
        k_ref[:, hh * dh:(hh + 1) * dh] = _rms(k[:, hh * dh:(hh + 1) * dh], gk).astype(k_ref.dtype)

    v_t = lax.dot_general(wvt_ref[...], h, NT_DIMS, preferred_element_type=F32)
    ones = jnp.ones((MOBA_ONES_ROWS, v_t.shape[1]), vt_ref.dtype)
    parts = []
    for hh in range(nheads):
        parts += [v_t[hh * dh:(hh + 1) * dh, :].astype(vt_ref.dtype), ones]
    v_t = jnp.concatenate(parts, axis=0)
    for c in range(qt_ref.shape[0]):
        qt_ref[c] = q_t[:, c * blk:(c + 1) * blk]
        vt_ref[c] = v_t[:, c * blk:(c + 1) * blk]

    log_a = (jnp.minimum(z, 0.0) - jnp.log(1.0 + jnp.exp(-jnp.abs(z)))) * (1.0 / GLA_GATE_TAU)
    la_hi = log_a.astype(BF16)
    lah_ref[...] = la_hi
    lal_ref[...] = (log_a - la_hi.astype(F32)).astype(BF16)


def _moba_proj(h, wq_t, wk, wv_t, gq, gk, w_lr, w2, b, li, tm=1024, blk=MOBA_BLOCK):
    t, d = h.shape
    n = wk.shape[2]
    n_la = w2.shape[2]
    full = lambda i: (0, 0)
    resident = lambda shape: pl.BlockSpec(shape, full, pipeline_mode=pl.Buffered(1))
    nv = n // MOBA_DH * MOBA_DV_AUG
    q_spec = pl.BlockSpec((tm // blk, n, blk), lambda i: (i, 0, 0))
    v_spec = pl.BlockSpec((tm // blk, nv, blk), lambda i: (i, 0, 0))
    q_shape = jax.ShapeDtypeStruct((t // blk, n, blk), BF16)
    v_shape = jax.ShapeDtypeStruct((t // blk, nv, blk), BF16)
    return pl.pallas_call(
        functools.partial(_moba_proj_kernel, scale=MOBA_DH ** -0.5 * LOG2E),
        grid=(t // tm,),
        in_specs=[pl.BlockSpec((tm, d), lambda i: (i, 0)),
                  _layer_weight(wq_t, li), _layer_weight(wk, li), _layer_weight(wv_t, li),
                  resident((MOBA_DH, 1)), resident((1, MOBA_DH)),
                  _layer_weight(w_lr, li), _layer_weight(w2, li), resident((1, n_la))],
        out_specs=[q_spec, pl.BlockSpec((tm, n), lambda i: (i, 0)), v_spec,
                   pl.BlockSpec((tm, n_la), lambda i: (i, 0)),
                   pl.BlockSpec((tm, n_la), lambda i: (i, 0))],
        out_shape=[q_shape, jax.ShapeDtypeStruct((t, n), BF16), v_shape,
                   jax.ShapeDtypeStruct((t, n_la), BF16), jax.ShapeDtypeStruct((t, n_la), BF16)],
        compiler_params=_cparams("parallel"),
        name="moba_qkv_proj",
    )(h, wq_t, wk, wv_t, gq.reshape(MOBA_DH, 1), gk.reshape(1, MOBA_DH), w_lr, w2, b)


def _gla_kernel(h_ref, lah_ref, lal_ref, wqk_ref, wv_ref, wr_ref, gn_ref, o_ref,
                st_ref, tri_ref, *, rb):
    n_k = GLA_HEADS * GLA_DK
    h = h_ref[...]
    qk_all = jnp.dot(h, wqk_ref[...], preferred_element_type=F32)

    c = GLA_CHUNK
    nc = rb // c
    shift = c.bit_length() - 1
    row = lax.broadcasted_iota(jnp.int32, (rb, rb), 0)
    col = lax.broadcasted_iota(jnp.int32, (rb, rb), 1)
    same_chunk = (row >> shift) == (col >> shift)
    causal = same_chunk & (row >= col)

    @pl.when(pl.program_id(1) == 0)
    def _():
        st_ref[...] = jnp.zeros_like(st_ref)
        tri_ref[...] = causal.astype(BF16)

    tri = tri_ref[...]
    heads = range(GLA_HEADS)
    kcs = [slice(hh * GLA_DK, (hh + 1) * GLA_DK) for hh in heads]
    vcs = [slice(hh * GLA_DV, (hh + 1) * GLA_DV) for hh in heads]
    b = (jnp.dot(tri, lah_ref[...], preferred_element_type=F32)
         + jnp.dot(tri, lal_ref[...], preferred_element_type=F32))
    v_all = jnp.dot(h, wv_ref[...], preferred_element_type=F32).astype(BF16)
    g_r = jnp.dot(h, wr_ref[...], preferred_element_type=F32)
    gr_all = g_r * _sigmoid(g_r)
    b_last = jnp.concatenate(
        [jnp.broadcast_to(b[ci * c + c - 1:ci * c + c, :], (c, b.shape[1])) for ci in range(nc)],
        axis=0)
    q_all = qk_all[:, :n_k]
    k_all = qk_all[:, n_k:]
    decay = jnp.exp(b)
    qd = (q_all * decay * (GLA_DK ** -0.5)).astype(BF16)
    kd = (k_all * jnp.exp(-b)).astype(BF16)
    kl = (k_all * jnp.exp(b_last - b)).astype(BF16)
    a = [lax.dot_general(qd[:, kcs[hh]], kd[:, kcs[hh]], NT_DIMS, preferred_element_type=F32)
         for hh in heads]
    chunks = [slice(ci * c, (ci + 1) * c) for ci in range(nc)]
    kv_t = [[lax.dot_general(v_all[rs, vcs[hh]], kl[rs, kcs[hh]], TN_DIMS,
                             preferred_element_type=F32) for hh in heads] for rs in chunks]
    o_intra = [jnp.dot(jnp.where(causal, a[hh], 0.0).astype(BF16), v_all[:, vcs[hh]],
                       preferred_element_type=F32) for hh in heads]
    st = [st_ref[hh] for hh in heads]
    o_inter = []
    for ci, rs in enumerate(chunks):
        o_inter.append([lax.dot_general(qd[rs, kcs[hh]], st[hh].astype(BF16), NT_DIMS,
                                        preferred_element_type=F32) for hh in heads])
        last = ci * c + c - 1
        st = [st[hh] * decay[last:last + 1, kcs[hh]] + kv_t[ci][hh] for hh in heads]
    for hh in heads:
        st_ref[hh] = st[hh]
    for ci, rs in enumerate(chunks):
        for hh in heads:
            o = o_intra[hh][rs] + o_inter[ci][hh]
            o_ref[rs, vcs[hh]] = (_rms(o, gn_ref[hh]) * gr_all[rs, vcs[hh]]).astype(o_ref.dtype)


def _gla(h, la_hi, la_lo, w_qk, w_v, w_r, gnorm, li, batch, seq, rb=GLA_ROW_BLOCK):
    t, d = h.shape
    nblk = seq // rb
    nh = GLA_HEADS
    rows = lambda bi, s: (bi * nblk + s, 0)
    return pl.pallas_call(
        functools.partial(_gla_kernel, rb=rb),
        grid=(batch, nblk),
        in_specs=[pl.BlockSpec((rb, d), rows), pl.BlockSpec((rb, la_hi.shape[1]), rows),
                  pl.BlockSpec((rb, la_lo.shape[1]), rows),
                  _layer_weight(w_qk, li), _layer_weight(w_v, li), _layer_weight(w_r, li),
                  pl.BlockSpec((nh, 1, GLA_DV), lambda bi, s: (0, 0, 0))],
        out_specs=pl.BlockSpec((rb, nh * GLA_DV), lambda bi, s: (bi * nblk + s, 0)),
        out_shape=jax.ShapeDtypeStruct((t, nh * GLA_DV), BF16),
        scratch_shapes=[pltpu.VMEM((nh, GLA_DV, GLA_DK), F32),
                        pltpu.VMEM((rb, rb), BF16)],
        compiler_params=_cparams("parallel", "arbitrary"),
        name="gla_branch",
    )(h, la_hi, la_lo, w_qk, w_v, w_r, gnorm.reshape(nh, 1, GLA_DV))


def _alibi_split(slope):
    c = slope * LOG2E
    c_hi = c.astype(BF16).astype(F32)
    return c_hi, c - c_hi


def _moba_build_e(slope, kaug_ref, nb):
    blk, dh = MOBA_BLOCK, MOBA_DH
    col = lax.broadcasted_iota(jnp.int32, (blk, dh), 1)
    r_key = lax.broadcasted_iota(jnp.int32, (blk, dh), 0).astype(F32)
    c_hi, c_lo = _alibi_split(slope[:, :dh])
    base = jnp.where((col == nb) | (col == nb + 4), c_hi, 0.0)
    base = jnp.where((col == nb + 1) | (col == nb + 5), c_lo, base)
    base = jnp.where((col == nb + 2) | (col == nb + 3), r_key, base)
    is_offset = (col == nb + 6) | (col == nb + 7)
    for n in range(nb):
        e = jnp.where(col == n, 1.0, jnp.where(is_offset, float(n * blk), base))
        kaug_ref[n * blk:(n + 1) * blk, dh:] = e.astype(BF16)


def _moba_load_keys(k, kaug_ref, kmh_ref, kml_ref, nb):
    blk, dh = MOBA_BLOCK, MOBA_DH
    ones = jnp.ones((SUBLANES, blk), BF16)
    sums = []
    for n in range(nb):
        kn = k[n * blk:(n + 1) * blk, :]
        kaug_ref[n * blk:(n + 1) * blk, :dh] = kn
        sums.append(jnp.dot(ones, kn, preferred_element_type=F32)[:1])
    km = jnp.concatenate(sums, axis=0) * (1.0 / blk)
    km_hi = km.astype(BF16)
    kmh_ref[...] = km_hi
    kml_ref[...] = (km - km_hi.astype(F32)).astype(BF16)


def _moba_alibi_rows(slope, i, nb):
    blk, dh = MOBA_BLOCK, MOBA_DH
    ridx = lax.broadcasted_iota(jnp.int32, (dh - nb, blk), 0) + nb
    r_t = lax.broadcasted_iota(jnp.int32, (dh - nb, blk), 1).astype(F32)
    off = (i * blk).astype(F32)
    c_hi, c_lo = _alibi_split(slope)
    rest = jnp.where((ridx == nb) | (ridx == nb + 1), -r_t, 0.0)
    rest = jnp.where((ridx == nb + 2) | (ridx == nb + 6), c_hi, rest)
    rest = jnp.where((ridx == nb + 3) | (ridx == nb + 7), c_lo, rest)
    rest = jnp.where((ridx == nb + 4) | (ridx == nb + 5), -off, rest)
    return rest.astype(BF16)


def _moba_select(g_t, i, nb):
    blk = MOBA_BLOCK
    nidx = lax.broadcasted_iota(jnp.int32, (nb, blk), 0)
    valid = nidx < i
    g = jnp.where(valid, g_t, -jnp.inf)
    rank = jnp.zeros((nb, blk), jnp.int32)
    for m in range(nb):
        gm = g[m:m + 1, :]
        beats = (gm > g) | ((gm == g) & (nidx > m))
        rank = rank + beats.astype(jnp.int32)
    return jnp.where(valid & (rank < MOBA_TOPK), 0.0, NEG_BIG).astype(BF16)


def _moba_kernel(qt_ref, k_ref, vt_ref, slope_ref, o_ref, kaug_ref, qaug_ref, kmh_ref, kml_ref,
                 s0_ref, s1_ref, acc_ref, *, nb, hp):
    blk, dh, dva = MOBA_BLOCK, MOBA_DH, MOBA_DV_AUG
    i = pl.program_id(2)
    hs = [slice(hh * dh, (hh + 1) * dh) for hh in range(hp)]
    vs = [slice(hh * dva, (hh + 1) * dva) for hh in range(hp)]

    @pl.when((pl.program_id(1) == 0) & (i == 0))
    def _():
        for hh in range(hp):
            _moba_build_e(slope_ref[hh], kaug_ref.at[hh], nb)

    @pl.when(i == 0)
    def _():
        for hh in range(hp):
            _moba_load_keys(k_ref[:, hs[hh]], kaug_ref.at[hh], kmh_ref.at[hh], kml_ref.at[hh], nb)

    own_rows = pl.ds(pl.multiple_of(i * blk, blk), blk)
    alibi = [_moba_alibi_rows(slope_ref[hh], i, nb) for hh in range(hp)]
    no_sel = jnp.zeros((nb, blk), BF16)
    owns = [jnp.dot(kaug_ref[hh, own_rows, :],
                    jnp.concatenate([qt_ref[hs[hh], :], no_sel, alibi[hh]], axis=0),
                    preferred_element_type=F32)
            for hh in range(hp)]
    gates = [jnp.dot(kmh_ref[hh], qt_ref[hs[hh], :], preferred_element_type=F32)
             + jnp.dot(kml_ref[hh], qt_ref[hs[hh], :], preferred_element_type=F32)
             for hh in range(hp)]
    for hh in range(hp):
        qaug_ref[hh] = jnp.concatenate(
            [qt_ref[hs[hh], :], _moba_select(gates[hh], i, nb), alibi[hh]], axis=0)

    def past_scores(hh, j):
        rows = pl.ds(pl.multiple_of(j * blk, blk), blk)
        s = jnp.dot(kaug_ref[hh, rows, :], qaug_ref[hh], preferred_element_type=F32)
        return s, jnp.max(s, axis=0, keepdims=True)

    def update(hh, s, s_max, j, m):
        m_new = jnp.maximum(m, s_max)
        alpha = jnp.exp2(m - m_new)
        p = jnp.exp2((s - m_new).astype(BF16))
        pv = jnp.dot(vt_ref[j, vs[hh], :], p, preferred_element_type=F32)
        acc_ref[hh] = alpha * acc_ref[hh] + pv
        return m_new

    kk = lax.broadcasted_iota(jnp.int32, (blk, blk), 0)
    qq = lax.broadcasted_iota(jnp.int32, (blk, blk), 1)
    ms = []
    for hh in range(hp):
        acc_ref[hh] = jnp.zeros((dva, blk), F32)
        own = jnp.where(qq >= kk, owns[hh], NEG_BIG)
        ms.append(update(hh, own, jnp.max(own, axis=0, keepdims=True), i,
                         jnp.full((1, blk), -jnp.inf, F32)))
    carry = []
    for hh in range(hp):
        s, s_max = past_scores(hh, 0)
        s0_ref[hh] = s
        carry.append((ms[hh], s_max))

    def body(jj, carry):
        j0 = 2 * jj
        j2 = jnp.minimum(j0 + 2, nb - 1)
        ms = [c[0] for c in carry]
        max0 = [c[1] for c in carry]
        max1 = []
        for hh in range(hp):
            s, s_max = past_scores(hh, j0 + 1)
            s1_ref[hh] = s
            max1.append(s_max)
        for hh in range(hp):
            ms[hh] = update(hh, s0_ref[hh], max0[hh], j0, ms[hh])
        for hh in range(hp):
            s, max0[hh] = past_scores(hh, j2)
            s0_ref[hh] = s
        for hh in range(hp):
            ms[hh] = update(hh, s1_ref[hh], max1[hh], j0 + 1, ms[hh])
        return tuple(zip(ms, max0))

    def last_block(_, carry):
        return tuple((update(hh, s0_ref[hh], carry[hh][1], i - 1, carry[hh][0]),
                      carry[hh][1]) for hh in range(hp))

    def trips(n_pairs):
        def many(kk, c):
            for r in range(n_pairs):
                c = body(n_pairs * kk + r, c)
            return c
        return many

    pairs_done = 0
    carry = tuple(carry)
    for n_pairs in (4, 2, 1):
        n_trips = (i // 2 - pairs_done) // n_pairs
        start = pairs_done // n_pairs
        carry = lax.fori_loop(start, start + n_trips, trips(n_pairs), carry)
        pairs_done = pairs_done + n_trips * n_pairs
    lax.fori_loop(0, i % 2, last_block, carry)
    for hh in range(hp):
        acc = acc_ref[hh]
        o_ref[:, hs[hh]] = (acc[:dh] / acc[dh:dh + 1]).T.astype(o_ref.dtype)


def _moba(q_t, k, v_t, slopes, batch, seq, hp=MOBA_HEADS_PER_STEP):
    t = k.shape[0]
    h, dh, dva, blk = MOBA_HEADS, MOBA_DH, MOBA_DV_AUG, MOBA_BLOCK
    nb = seq // blk
    hg = h // hp
    return pl.pallas_call(
        functools.partial(_moba_kernel, nb=nb, hp=hp),
        grid=(hg, batch, nb),
        in_specs=[pl.BlockSpec((None, hp * dh, blk), lambda g, b, i: (b * nb + i, g, 0)),
                  pl.BlockSpec((seq, hp * dh), lambda g, b, i: (b, g)),
                  pl.BlockSpec((nb, hp * dva, blk), lambda g, b, i: (b, g, 0)),
                  pl.BlockSpec((hp, 1, blk), lambda g, b, i: (g, 0, 0))],
        out_specs=pl.BlockSpec((blk, hp * dh), lambda g, b, i: (b * nb + i, g)),
        out_shape=jax.ShapeDtypeStruct((t, h * dh), BF16),
        scratch_shapes=[pltpu.VMEM((hp, seq, 2 * dh), BF16),
                        pltpu.VMEM((hp, 2 * dh, blk), BF16),
                        pltpu.VMEM((hp, nb, dh), BF16),
                        pltpu.VMEM((hp, nb, dh), BF16),
                        pltpu.VMEM((hp, blk, blk), F32),
                        pltpu.VMEM((hp, blk, blk), F32),
                        pltpu.VMEM((hp, dva, blk), F32)],
        compiler_params=_cparams("parallel", "arbitrary", "arbitrary"),
        name="moba_attn",
    )(q_t, k, v_t, slopes)


def _merge_kernel(x_ref, h_ref, oa_ref, ob_ref, wg_ref, wa_ref, wb_ref, wo_ref, o_ref):
    d = x_ref.shape[1]
    gates = _sigmoid(jnp.dot(h_ref[...], wg_ref[...], preferred_element_type=F32))
    ya = jnp.dot(oa_ref[...], wa_ref[...], preferred_element_type=F32)
    yb = jnp.dot(ob_ref[...], wb_ref[...], preferred_element_type=F32)
    y = gates[:, :d] * ya + gates[:, d:] * yb
    o_ref[...] = x_ref[...] + jnp.dot(y.astype(BF16), wo_ref[...], preferred_element_type=F32)


def _merge(x, h, oa, ob, wg, wa, wb, wo, li, tm=512):
    t, d = x.shape
    row = lambda i: (i, 0)
    return pl.pallas_call(
        _merge_kernel,
        grid=(t // tm,),
        in_specs=[pl.BlockSpec((tm, d), row), pl.BlockSpec((tm, d), row),
                  pl.BlockSpec((tm, d), row), pl.BlockSpec((tm, d), row),
                  _layer_weight(wg, li), _layer_weight(wa, li),
                  _layer_weight(wb, li), _layer_weight(wo, li)],
        out_specs=pl.BlockSpec((tm, d), row),
        out_shape=jax.ShapeDtypeStruct((t, d), F32),
        compiler_params=_cparams("parallel"),
        name="merge_out_proj",
    )(x, h, oa, ob, wg, wa, wb, wo)


def _mlp_ple_kernel(x_ref, p_ref, gm_ref, wu_ref, wd_ref, gp_ref, wg_ref, wp_ref, gn_ref,
                    o_ref, *maybe_h_ref, nchunk):
    x = x_ref[...]
    h2 = _rms(x, gm_ref[...]).astype(BF16)
    tf = wu_ref.shape[1] // nchunk
    acc = x
    for c in range(nchunk):
        up = jnp.dot(h2, wu_ref[:, c * tf:(c + 1) * tf], preferred_element_type=F32)
        act = jnp.square(jnp.maximum(up, 0.0)).astype(BF16)
        acc = acc + jnp.dot(act, wd_ref[c * tf:(c + 1) * tf, :], preferred_element_type=F32)
    hn = _rms(acc, gp_ref[...]).astype(BF16)
    gate = _sigmoid(jnp.dot(hn, wg_ref[...], preferred_element_type=F32))
    e = jnp.dot(p_ref[...].astype(BF16), wp_ref[...], preferred_element_type=F32)
    xo = acc + gate * e
    o_ref[...] = xo
    if maybe_h_ref:
        maybe_h_ref[0][...] = _rms(xo, gn_ref[...]).astype(BF16)


def _mlp_ple(x, p, g_mlp, wu, wd, g_ple, wg, wp, g_next, li, tm=512, nchunk=4):
    t, d = x.shape
    pd = p.shape[2]
    row = lambda i: (i, 0)
    full = lambda i: (0, 0)
    resident = lambda shape: pl.BlockSpec(shape, full, pipeline_mode=pl.Buffered(1))
    emit_next = g_next is not None
    out_shape = [jax.ShapeDtypeStruct((t, d), F32)]
    out_specs = [pl.BlockSpec((tm, d), row)]
    if emit_next:
        out_shape.append(jax.ShapeDtypeStruct((t, d), BF16))
        out_specs.append(pl.BlockSpec((tm, d), row))
    gn = (g_next if emit_next else g_ple).reshape(1, d)
    res = pl.pallas_call(
        functools.partial(_mlp_ple_kernel, nchunk=nchunk),
        grid=(t // tm,),
        in_specs=[pl.BlockSpec((tm, d), row), pl.BlockSpec((None, tm, pd), lambda i: (li, i, 0)),
                  resident((1, d)), _layer_weight(wu, li), _layer_weight(wd, li),
                  resident((1, d)), _layer_weight(wg, li), _layer_weight(wp, li),
                  resident((1, d))],
        out_specs=out_specs,
        out_shape=out_shape,
        compiler_params=_cparams("parallel"),
        name="mlp_ple",
    )(x, p, g_mlp.reshape(1, d), wu, wd, g_ple.reshape(1, d), wg, wp, gn)
    return (res[0], res[1]) if emit_next else (res[0], None)


def kernel(x, p, norm_mix, w_in, gla_gate_w2, gla_gate_b, gla_out_norm, moba_q_norm,
           moba_k_norm, w_branch_a, w_branch_b, w_out, norm_mlp, w_up, w_down,
           norm_ple, w_ple_gate, w_ple):
    batch, seq, d = x.shape
    depth = w_in.shape[0]
    t = batch * seq
    x = x.reshape(t, d)

    slopes = 2.0 ** (-8.0 * jnp.arange(1, MOBA_HEADS + 1, dtype=F32) / MOBA_HEADS)
    slopes = jnp.broadcast_to(slopes[:, None, None], (MOBA_HEADS, 1, MOBA_BLOCK))

    w_gqk, w_gv, w_lr, w_gr, w_mq_t, w_mk, w_mv_t, w_gate = _split_w_in(w_in)
    w2 = jnp.pad(gla_gate_w2, ((0, 0), (0, LANES - GLA_GATE_RANK), (0, 0))).astype(BF16)
    w_a, w_b, w_o = (w.astype(BF16) for w in (w_branch_a, w_branch_b, w_out))
    w_u, w_d, w_pg, w_pe = (w.astype(BF16) for w in (w_up, w_down, w_ple_gate, w_ple))
    p = p.reshape(depth, t, -1)

    h = _norm_cast(x, norm_mix[0])
    for li in range(depth):
        mq_t, mk, mv_t, la_hi, la_lo = _moba_proj(
            h, w_mq_t, w_mk, w_mv_t, moba_q_norm[li], moba_k_norm[li], w_lr, w2,
            gla_gate_b[li].reshape(1, -1), li)

        oa = _gla(h, la_hi, la_lo, w_gqk, w_gv, w_gr, gla_out_norm[li], li, batch, seq)
        ob = _moba(mq_t, mk, mv_t, slopes, batch, seq)

        x = _merge(x, h, oa, ob, w_gate, w_a, w_b, w_o, li)
        g_next = norm_mix[li + 1] if li + 1 < depth else None
        x, h = _mlp_ple(x, p, norm_mlp[li], w_u, w_d, norm_ple[li], w_pg, w_pe, g_next, li)
    return x.reshape(batch, seq, d)
```
